```python
import jax, jax.numpy as jnp
from jax import lax
import numpy as np

D_MODEL = 1024
BATCH = 4
SEQ = 4096
DEPTH = 1
DEC_BATCH = 32
DEC_SEQ = 32
PAST_LEN = 1024

CHUNK = 64
PLE_DIM = 256
M_HEADS = 4
M_INNER = 2 * D_MODEL
M_HD = M_INNER // M_HEADS
CONV_W = 4
H_EXPAND = 128
H_WIDTH = D_MODEL
H_HEADS = H_WIDTH // H_EXPAND
IN_SIZES = (M_INNER, M_INNER, M_INNER, M_HEADS, M_HEADS, H_WIDTH, H_WIDTH, H_WIDTH, H_WIDTH, D_MODEL, D_MODEL)
N_IN = 3 * M_INNER + 2 * M_HEADS + 4 * H_WIDTH + 2 * D_MODEL
EPS = 1e-6

kernel_name = 'mlstm_hgrn2_gated_stream_step'


def _rmsnorm(x, g):
    xf = x.astype(jnp.float32)
    y = xf * lax.rsqrt(jnp.mean(xf * xf, axis=-1, keepdims=True) + EPS)
    return (y * g.astype(jnp.float32)).astype(x.dtype)


def _head_layernorm(h, g):
    mu = jnp.mean(h, axis=-1, keepdims=True)
    d = h - mu
    y = d * lax.rsqrt(jnp.mean(d * d, axis=-1, keepdims=True) + EPS)
    return y.reshape(h.shape[0], h.shape[1], -1) * g


def _head_rmsnorm(h, g):
    y = h * lax.rsqrt(jnp.mean(h * h, axis=-1, keepdims=True) + EPS)
    return y.reshape(h.shape[0], h.shape[1], -1) * g


def _causal_conv(u, buf, w, b):
    T = u.shape[1]
    xp = jnp.concatenate([buf, u], axis=1)
    y = b + xp[:, 0:T] * w[0]
    for j in range(1, CONV_W):
        y = y + xp[:, j:j + T] * w[j]
    return y, xp[:, -(CONV_W - 1):]


def _to_chunks(a, L):
    a = a.reshape(a.shape[:2] + (a.shape[2] // L, L) + a.shape[3:])
    return jnp.moveaxis(a, 2, 0)


def _from_chunks(a):
    a = jnp.moveaxis(a, 0, 2)
    return a.reshape(a.shape[:2] + (a.shape[2] * a.shape[3],) + a.shape[4:])


def _mlstm_chunk(carry, inp):
    C, n, m = carry
    q, k, v, ig, lf = inp
    L = q.shape[2]
    causal = jnp.tril(jnp.ones((L, L), dtype=bool))
    b = jnp.cumsum(lf, axis=-1)
    logd = jnp.where(causal, b[..., :, None] - b[..., None, :] + ig[..., None, :], -jnp.inf)
    m_prev = b + m[..., None]
    m_t = jnp.maximum(m_prev, jnp.max(logd, axis=-1))
    dmat = jnp.exp(logd - m_t[..., None])
    w_inter = jnp.exp(m_prev - m_t)
    s = jnp.einsum('bhtd,bhsd->bhts', q, k) * dmat
    num = w_inter[..., None] * jnp.einsum('bhed,bhtd->bhte', C, q) + jnp.einsum('bhts,bhse->bhte', s, v)
    den = w_inter * jnp.einsum('bhd,bhtd->bht', n, q) + jnp.sum(s, axis=-1)
    h = num / jnp.maximum(jnp.abs(den), jnp.exp(-m_t))[..., None]
    m_last = m_t[..., -1]
    w_last = jnp.exp(b[..., -1:] - b + ig - m_last[..., None])
    decay = jnp.exp(m_prev[..., -1] - m_last)
    C_new = decay[..., None, None] * C + jnp.einsum('bhs,bhse,bhsd->bhed', w_last, v, k)
    n_new = decay[..., None] * n + jnp.einsum('bhs,bhsd->bhd', w_last, k)
    return (C_new, n_new, m_last), h


def _hgrn_chunk(S, inp):
    q, k, v, lf = inp
    L = q.shape[2]
    causal = jnp.tril(jnp.ones((L, L), dtype=bool))[:, :, None]
    g = jnp.cumsum(lf, axis=2)
    dec = jnp.exp(jnp.where(causal, g[:, :, :, None, :] - g[:, :, None, :, :], -jnp.inf))
    a = jnp.einsum('bhtc,bhtsc,bhsc->bhts', q, dec, k)
    o = jnp.einsum('bhtc,bhce->bhte', q * jnp.exp(g), S) + jnp.einsum('bhts,bhse->bhte', a, v)
    g_last = g[:, :, -1:, :]
    S_new = jnp.exp(g_last[:, :, 0, :])[..., None] * S + jnp.einsum('bhsc,bhse->bhce', k * jnp.exp(g_last - g), v)
    return S_new, o


def _mlstm(q, k, v, ig, lf, C, n, m):
    L = min(CHUNK, q.shape[2])
    xs = (_to_chunks(q, L), _to_chunks(k, L), _to_chunks(v, L), _to_chunks(ig, L), _to_chunks(lf, L))
    (C, n, m), h = lax.scan(_mlstm_chunk, (C, n, m), xs)
    return _from_chunks(h), C, n, m


def _hgrn(q, k, v, lf, S):
    L = min(CHUNK, q.shape[2])
    xs = (_to_chunks(q, L), _to_chunks(k, L), _to_chunks(v, L), _to_chunks(lf, L))
    S, o = lax.scan(_hgrn_chunk, S, xs)
    return _from_chunks(o), S


def _split_points():
    pts, acc = [], 0
    for s in IN_SIZES[:-1]:
        acc += s
        pts.append(acc)
    return pts


def _trunk(x, p, conv0, C0, n0, m0, S0, norm_g, w_in, b_ig, b_fg, conv_w, conv_b, w_qm, w_km,
           mnorm_g, m_skip, w_brm, hgrn_lb, hnorm_g, w_brh, w_out, w_ple, w_pg, final_g):
    f32 = jnp.float32
    B_, T, _ = x.shape
    lb_all = jnp.cumsum(jax.nn.softmax(hgrn_lb.astype(f32), axis=0), axis=0)
    h = x
    convs, Cs, ns, ms, Ss = [], [], [], [], []
    for l in range(DEPTH):
        xn = _rmsnorm(h, norm_g[l])
        proj = xn @ w_in[l]
        u, v_m, z_m, ig_pre, fg_pre, q_h, f_h, i_h, g_h, ga, gb = jnp.split(proj, _split_points(), axis=-1)
        c, conv_new = _causal_conv(u, conv0[l].astype(u.dtype), conv_w[l], conv_b[l])
        c = jax.nn.silu(c).astype(f32)
        ch = c.reshape(B_, T, M_HEADS, M_HD)
        qm = jnp.einsum('bthd,hde->bhte', ch, w_qm[l].astype(f32))
        km = jnp.einsum('bthd,hde->bhte', ch, w_km[l].astype(f32)) * (M_HD ** -0.5)
        vm = v_m.astype(f32).reshape(B_, T, M_HEADS, M_HD).transpose(0, 2, 1, 3)
        ig = (ig_pre.astype(f32) + b_ig[l].astype(f32)).transpose(0, 2, 1)
        lf = jax.nn.log_sigmoid(fg_pre.astype(f32) + b_fg[l].astype(f32)).transpose(0, 2, 1)
        hm, C_new, n_new, m_new = _mlstm(qm, km, vm, ig, lf, C0[l].astype(f32), n0[l].astype(f32), m0[l].astype(f32))
        hm = _head_layernorm(hm.transpose(0, 2, 1, 3), mnorm_g[l].astype(f32))
        hm = (hm + m_skip[l].astype(f32) * c) * jax.nn.silu(z_m.astype(f32))
        ya = hm.astype(x.dtype) @ w_brm[l]
        lb = lb_all[l]
        f_lin = f_h.astype(f32)
        lfh = jnp.log(lb + (1.0 - lb) * jax.nn.sigmoid(f_lin))
        kh = (1.0 - lb) * jax.nn.sigmoid(-f_lin)
        qh = jax.nn.silu(q_h.astype(f32))
        to_heads = lambda a: a.reshape(B_, T, H_HEADS, H_EXPAND).transpose(0, 2, 1, 3)
        oh, S_new = _hgrn(to_heads(qh), to_heads(kh), to_heads(i_h.astype(f32)), to_heads(lfh), S0[l].astype(f32))
        oh = _head_rmsnorm(oh.transpose(0, 2, 1, 3), hnorm_g[l].astype(f32)) * jax.nn.silu(g_h.astype(f32))
        yb = oh.astype(x.dtype) @ w_brh[l]
        y = jax.nn.sigmoid(ga) * ya + jax.nn.sigmoid(gb) * yb
        h = h + y @ w_out[l]
        h = h + jax.nn.sigmoid(h @ w_pg[l]) * (p[l] @ w_ple[l])
        convs.append(conv_new); Cs.append(C_new); ns.append(n_new); ms.append(m_new); Ss.append(S_new)
    out = _rmsnorm(h, final_g)
    return out, jnp.stack(convs), jnp.stack(Cs), jnp.stack(ns), jnp.stack(ms), jnp.stack(Ss)


def setup_inputs(seed: int = 0) -> dict:
    key = jax.random.key(seed)
    ks = jax.random.split(key, 32)
    nrm = lambda k, s, sc: jax.random.normal(k, s, jnp.float32) * sc
    return {
        'x_prompt': nrm(ks[0], (BATCH, SEQ, D_MODEL), 1.0),
        'x_sample': nrm(ks[1], (DEC_BATCH, DEC_SEQ, D_MODEL), 1.0),
        'state_conv': nrm(ks[2], (DEPTH, DEC_BATCH, CONV_W - 1, M_INNER), 1.0),
        'state_mlstm_C': nrm(ks[3], (DEPTH, DEC_BATCH, M_HEADS, M_HD, M_HD), 0.05),
        'state_mlstm_n': nrm(ks[4], (DEPTH, DEC_BATCH, M_HEADS, M_HD), 0.05),
        'state_mlstm_m': nrm(ks[5], (DEPTH, DEC_BATCH, M_HEADS), 1.0),
        'state_hgrn': nrm(ks[6], (DEPTH, DEC_BATCH, H_HEADS, H_EXPAND, H_EXPAND), 0.1),
        'p_prompt': nrm(ks[7], (DEPTH, BATCH, SEQ, PLE_DIM), 1.0),
        'p_sample': nrm(ks[8], (DEPTH, DEC_BATCH, DEC_SEQ, PLE_DIM), 1.0),
        'norm_g': 1.0 + nrm(ks[9], (DEPTH, D_MODEL), 0.1),
        'w_in': nrm(ks[10], (DEPTH, D_MODEL, N_IN), D_MODEL ** -0.5),
        'b_ig': nrm(ks[11], (DEPTH, M_HEADS), 0.1),
        'b_fg': jnp.linspace(3.0, 6.0, M_HEADS)[None, :] + nrm(ks[12], (DEPTH, M_HEADS), 0.1),
        'conv_w': nrm(ks[13], (DEPTH, CONV_W, M_INNER), CONV_W ** -0.5),
        'conv_b': nrm(ks[14], (DEPTH, M_INNER), 0.02),
        'w_qm': nrm(ks[15], (DEPTH, M_HEADS, M_HD, M_HD), M_HD ** -0.5),
        'w_km': nrm(ks[16], (DEPTH, M_HEADS, M_HD, M_HD), M_HD ** -0.5),
        'mnorm_g': 1.0 + nrm(ks[17], (DEPTH, M_INNER), 0.1),
        'm_skip': 1.0 + nrm(ks[18], (DEPTH, M_INNER), 0.1),
        'w_brm': nrm(ks[19], (DEPTH, M_INNER, D_MODEL), M_INNER ** -0.5),
        'hgrn_lb': nrm(ks[20], (DEPTH + 1, H_WIDTH), 0.1),
        'hnorm_g': 1.0 + nrm(ks[21], (DEPTH, H_WIDTH), 0.1),
        'w_brh': nrm(ks[22], (DEPTH, H_WIDTH, D_MODEL), H_WIDTH ** -0.5),
        'w_out': nrm(ks[23], (DEPTH, D_MODEL, D_MODEL), D_MODEL ** -0.5),
        'w_ple': nrm(ks[24], (DEPTH, PLE_DIM, D_MODEL), PLE_DIM ** -0.5),
        'w_pg': nrm(ks[25], (DEPTH, D_MODEL, D_MODEL), D_MODEL ** -0.5),
        'final_g': 1.0 + nrm(ks[26], (D_MODEL,), 0.1),
    }


def reference(x_prompt, x_sample, state_conv, state_mlstm_C, state_mlstm_n, state_mlstm_m, state_hgrn,
              p_prompt, p_sample, norm_g, w_in, b_ig, b_fg, conv_w, conv_b, w_qm, w_km, mnorm_g, m_skip,
              w_brm, hgrn_lb, hnorm_g, w_brh, w_out, w_ple, w_pg, final_g):
    f32 = jnp.float32
    B_ = x_prompt.shape[0]
    conv0 = jnp.zeros((DEPTH, B_, CONV_W - 1, M_INNER), x_prompt.dtype)
    C0 = jnp.zeros((DEPTH, B_, M_HEADS, M_HD, M_HD), f32)
    n0 = jnp.zeros((DEPTH, B_, M_HEADS, M_HD), f32)
    m0 = jnp.zeros((DEPTH, B_, M_HEADS), f32)
    S0 = jnp.zeros((DEPTH, B_, H_HEADS, H_EXPAND, H_EXPAND), f32)
    y_prompt, conv_p, C_p, n_p, m_p, S_p = _trunk(
        x_prompt, p_prompt, conv0, C0, n0, m0, S0, norm_g, w_in, b_ig, b_fg, conv_w, conv_b, w_qm, w_km,
        mnorm_g, m_skip, w_brm, hgrn_lb, hnorm_g, w_brh, w_out, w_ple, w_pg, final_g)
    y_sample, conv_s, C_s, n_s, m_s, S_s = _trunk(
        x_sample, p_sample, state_conv, state_mlstm_C, state_mlstm_n, state_mlstm_m, state_hgrn,
        norm_g, w_in, b_ig, b_fg, conv_w, conv_b, w_qm, w_km,
        mnorm_g, m_skip, w_brm, hgrn_lb, hnorm_g, w_brh, w_out, w_ple, w_pg, final_g)
    return (y_prompt, y_sample, conv_p, C_p, n_p, m_p, S_p, conv_s, C_s, n_s, m_s, S_s)
```

```python
import functools

import jax
import jax.numpy as jnp
from jax import lax
from jax.experimental import pallas as pl
from jax.experimental.pallas import tpu as pltpu

F32 = jnp.float32
BF16 = jnp.bfloat16

D_MODEL = 1024
PLE_DIM = 256
M_HEADS = 4
M_INNER = 2 * D_MODEL
M_HD = M_INNER // M_HEADS
CONV_W = 4
H_EXPAND = 128
H_WIDTH = D_MODEL
H_HEADS = H_WIDTH // H_EXPAND
EPS = 1e-6

N_MAIN = 3 * M_INNER + 4 * H_WIDTH + 2 * D_MODEL
GATE_LANES = 128
SUBLANES = 8
HGRN_BAND = SUBLANES
VMEM_LIMIT = 56 * 1024 * 1024


def _dot(a, b):
    return jnp.dot(a, b, preferred_element_type=F32)


def _dot_nt(a, b):
    return lax.dot_general(a, b, (((1,), (1,)), ((), ())), preferred_element_type=F32)


def _dot_tn(a, b):
    return lax.dot_general(a, b, (((0,), (0,)), ((), ())), preferred_element_type=F32)


def _split3(x):
    hi = x.astype(BF16)
    r = x - hi.astype(F32)
    mid = r.astype(BF16)
    lo = (r - mid.astype(F32)).astype(BF16)
    return hi, mid, lo


def _sigmoid(x):
    return 1.0 / (1.0 + jnp.exp(-x))


def _silu(x):
    return x * _sigmoid(x)


def _log_sigmoid(x):
    return jnp.minimum(x, 0.0) - jnp.log(1.0 + jnp.exp(-jnp.abs(x)))


def _tri(n, lower):
    r = lax.broadcasted_iota(jnp.int32, (n, n), 0)
    c = lax.broadcasted_iota(jnp.int32, (n, n), 1)
    return jnp.where((r >= c) if lower else (r <= c), 1.0, 0.0).astype(BF16)


def _proj_kernel(x_ref, g_ref, w_ref, wgh_ref, wgl_ref, o_ref, gate_ref, xh_sc, xl_sc):
    @pl.when(pl.program_id(1) == 0)
    def _():
        x = x_ref[...]
        xn = x * lax.rsqrt(jnp.mean(x * x, axis=-1, keepdims=True) + EPS) * g_ref[...]
        hi = xn.astype(BF16)
        lo = (xn - hi.astype(F32)).astype(BF16)
        xh_sc[...] = hi
        xl_sc[...] = lo
        gate_ref[...] = _dot(hi, wgh_ref[...]) + _dot(lo, wgh_ref[...]) + _dot(hi, wgl_ref[...])

    o_ref[...] = _dot(xh_sc[...], w_ref[...]).astype(BF16)


def _proj(x2, norm_g, w_main, wg_hi, wg_lo, tm, tn):
    n_tok = x2.shape[0]
    grid = (n_tok // tm, N_MAIN // tn)
    return pl.pallas_call(
        _proj_kernel,
        grid=grid,
        in_specs=[
            pl.BlockSpec((tm, D_MODEL), lambda i, j: (i, 0)),
            pl.BlockSpec((1, D_MODEL), lambda i, j: (0, 0)),
            pl.BlockSpec((D_MODEL, tn), lambda i, j: (0, j)),
            pl.BlockSpec((D_MODEL, GATE_LANES), lambda i, j: (0, 0)),
            pl.BlockSpec((D_MODEL, GATE_LANES), lambda i, j: (0, 0)),
        ],
        out_specs=[
            pl.BlockSpec((tm, tn), lambda i, j: (i, j)),
            pl.BlockSpec((tm, GATE_LANES), lambda i, j: (i, 0)),
        ],
        out_shape=[
            jax.ShapeDtypeStruct((n_tok, N_MAIN), BF16),
            jax.ShapeDtypeStruct((n_tok, GATE_LANES), F32),
        ],
        scratch_shapes=[pltpu.VMEM((tm, D_MODEL), BF16), pltpu.VMEM((tm, D_MODEL), BF16)],
        compiler_params=pltpu.CompilerParams(
            dimension_semantics=("arbitrary", "arbitrary"), vmem_limit_bytes=VMEM_LIMIT),
        name="proj",
    )(x2, norm_g, w_main, wg_hi, wg_lo)


def _mlstm_kernel(*refs, L, has_state):
    (pm_ref, gate_ref, gbias_ref, convw_ref, convb_ref, wq_ref, wk_ref, mng_ref, skip_ref) = refs[:9]
    if has_state:
        conv0_ref, C0_ref, n0_ref, m0_ref = refs[9:13]
        rest = refs[13:]
    else:
        rest = refs[9:]
    hm_ref, convo_ref, C_ref, n_ref, m_ref, uext_sc = rest
    c_idx = pl.program_id(1)
    TAIL = CONV_W - 1
    BASE = SUBLANES

    @pl.when(c_idx == 0)
    def _():
        if has_state:
            uext_sc[BASE - TAIL:BASE, :] = conv0_ref[...]
            C_ref[...] = C0_ref[...]
            n_ref[...] = n0_ref[...]
            m_ref[...] = m0_ref[...]
        else:
            uext_sc[BASE - TAIL:BASE, :] = jnp.zeros((TAIL, M_INNER), F32)
            C_ref[...] = jnp.zeros(C_ref.shape, F32)
            n_ref[...] = jnp.zeros(n_ref.shape, F32)
            m_ref[...] = jnp.zeros(m_ref.shape, F32)

    uext_sc[BASE:BASE + L, :] = pm_ref[:, 0:M_INNER].astype(F32)
    conv = convb_ref[...] + uext_sc[BASE - TAIL:BASE - TAIL + L, :] * convw_ref[0:1, :]
    for j in range(1, CONV_W):
        conv = conv + uext_sc[BASE - TAIL + j:BASE - TAIL + j + L, :] * convw_ref[j:j + 1, :]
    tail = uext_sc[BASE + L - TAIL:BASE + L, :]
    uext_sc[BASE - TAIL:BASE, :] = tail
    convo_ref[...] = tail
    c_act = _silu(conv)

    gates = gate_ref[...] + gbias_ref[...]
    lf_cols = _log_sigmoid(gates)
    gates_t = gates.T[0:2 * M_HEADS, :]
    lf_rows = _log_sigmoid(gates_t)
    tril, triu = _tri(L, True), _tri(L, False)
    b_cols = sum(_dot(tril, part) for part in _split3(lf_cols))
    b_rows = sum(_dot(part, triu) for part in _split3(lf_rows))
    row_i = lax.broadcasted_iota(jnp.int32, (L, L), 0)
    col_i = lax.broadcasted_iota(jnp.int32, (L, L), 1)
    causal = row_i >= col_i

    for h in range(M_HEADS):
        hs = slice(h * M_HD, (h + 1) * M_HD)
        ch = c_act[:, hs].astype(BF16)
        q = _dot(ch, wq_ref[h])
        k = _dot(ch, wk_ref[h]) * (M_HD ** -0.5)
        v = pm_ref[:, M_INNER + h * M_HD:M_INNER + (h + 1) * M_HD]
        q_bf = q.astype(BF16)
        k_bf = k.astype(BF16)

        ig_col = gates[:, h:h + 1]
        ig_row = gates_t[h:h + 1, :]
        b_col = b_cols[:, M_HEADS + h:M_HEADS + h + 1]
        b_row = b_rows[M_HEADS + h:M_HEADS + h + 1, :]
        m_c = m_ref[h:h + 1, 0:1]

        logd = jnp.where(causal, b_col - b_row + ig_row, -jnp.inf)
        m_prev = b_col + m_c
        m_t = jnp.maximum(m_prev, jnp.max(logd, axis=1, keepdims=True))
        dmat = jnp.exp(logd - m_t)
        w_inter = jnp.exp(m_prev - m_t)

        C = C_ref[h]
        n = n_ref[h]
        s_mat = _dot_nt(q_bf, k_bf) * dmat
        num = w_inter * _dot_nt(q_bf, C.astype(BF16)) + _dot(s_mat.astype(BF16), v)
        den = w_inter * jnp.sum(q * n, axis=1, keepdims=True) + jnp.sum(s_mat, axis=1, keepdims=True)
        hh = num * (1.0 / jnp.maximum(jnp.abs(den), jnp.exp(-m_t)))

        m_last = m_t[L - 1:L, :]
        w_last = jnp.exp(b_col[L - 1:L, :] - b_col + ig_col - m_last)
        decay = jnp.exp(m_prev[L - 1:L, :] - m_last)
        wv = (w_last * v.astype(F32)).astype(BF16)
        C_ref[h] = decay * C + _dot_tn(wv, k_bf)
        n_ref[h] = decay * n + jnp.sum(w_last * k, axis=0, keepdims=True)
        m_ref[h:h + 1, :] = jnp.broadcast_to(m_last, (1, m_ref.shape[1]))

        mu = jnp.mean(hh, axis=1, keepdims=True)
        d = hh - mu
        y = d * lax.rsqrt(jnp.mean(d * d, axis=1, keepdims=True) + EPS) * mng_ref[:, hs]
        z = pm_ref[:, 2 * M_INNER + h * M_HD:2 * M_INNER + (h + 1) * M_HD].astype(F32)
        hm_ref[:, hs] = ((y + skip_ref[:, hs] * c_act[:, hs]) * _silu(z)).astype(BF16)


def _mlstm(proj, gates, gbias, conv_w, conv_b, wq, wk, mnorm_g, m_skip, state, B, T, L):
    NC = T // L
    has_state = state is not None
    tok = lambda b, c: (b * NC + c, 0)
    const2 = lambda b, c: (0, 0)
    in_specs = [
        pl.BlockSpec((L, 3 * M_INNER), tok),
        pl.BlockSpec((L, GATE_LANES), tok),
        pl.BlockSpec((1, GATE_LANES), const2),
        pl.BlockSpec((CONV_W, M_INNER), const2),
        pl.BlockSpec((1, M_INNER), const2),
        pl.BlockSpec((M_HEADS, M_HD, M_HD), lambda b, c: (0, 0, 0)),
        pl.BlockSpec((M_HEADS, M_HD, M_HD), lambda b, c: (0, 0, 0)),
        pl.BlockSpec((1, M_INNER), const2),
        pl.BlockSpec((1, M_INNER), const2),
    ]
    args = [proj, gates, gbias, conv_w, conv_b, wq, wk, mnorm_g, m_skip]
    state_specs = [
        pl.BlockSpec((None, CONV_W - 1, M_INNER), lambda b, c: (b, 0, 0)),
        pl.BlockSpec((None, M_HEADS, M_HD, M_HD), lambda b, c: (b, 0, 0, 0)),
        pl.BlockSpec((None, M_HEADS, 1, M_HD), lambda b, c: (b, 0, 0, 0)),
        pl.BlockSpec((None, SUBLANES, GATE_LANES), lambda b, c: (b, 0, 0)),
    ]
    if has_state:
        in_specs += state_specs
        args += list(state)
    return pl.pallas_call(
        functools.partial(_mlstm_kernel, L=L, has_state=has_state),
        grid=(B, NC),
        in_specs=in_specs,
        out_specs=[pl.BlockSpec((L, M_INNER), tok)] + state_specs,
        out_shape=[
            jax.ShapeDtypeStruct((B * T, M_INNER), BF16),
            jax.ShapeDtypeStruct((B, CONV_W - 1, M_INNER), F32),
            jax.ShapeDtypeStruct((B, M_HEADS, M_HD, M_HD), F32),
            jax.ShapeDtypeStruct((B, M_HEADS, 1, M_HD), F32),
            jax.ShapeDtypeStruct((B, SUBLANES, GATE_LANES), F32),
        ],
        scratch_shapes=[pltpu.VMEM((L + 2 * SUBLANES, M_INNER), F32)],
        compiler_params=pltpu.CompilerParams(
            dimension_semantics=("arbitrary", "arbitrary"), vmem_limit_bytes=VMEM_LIMIT),
        name="mlstm_state" if has_state else "mlstm",
    )(*args)


def _hgrn_kernel(*refs, L, has_state):
    qf_ref, ig_ref, lb_ref, hng_ref = refs[:4]
    if has_state:
        S0_ref = refs[4]
        rest = refs[5:]
    else:
        rest = refs[4:]
    oh_ref, S_ref, st_sc = rest
    c_idx = pl.program_id(1)
    HD = H_EXPAND

    @pl.when(c_idx == 0)
    def _():
        for h in range(H_HEADS):
            if has_state:
                st_sc[h] = S0_ref[h].T
            else:
                st_sc[h] = jnp.zeros((HD, HD), F32)

    lbp = lb_ref[...]
    lbm = jnp.max(lbp, axis=0, keepdims=True)
    lbe = jnp.exp(lbp - lbm)
    lb = lbe[0:1, :] / jnp.sum(lbe, axis=0, keepdims=True)

    f = qf_ref[:, H_WIDTH:2 * H_WIDTH].astype(F32)
    e = jnp.exp(-jnp.abs(f))
    r = 1.0 / (1.0 + e)
    sig_pos = jnp.where(f >= 0, r, e * r)
    sig_neg = jnp.where(f >= 0, e * r, r)
    lf = jnp.log(lb + (1.0 - lb) * sig_pos)
    kk = (1.0 - lb) * sig_neg
    qa = _silu(qf_ref[:, 0:H_WIDTH].astype(F32))
    v_bf = ig_ref[:, 0:H_WIDTH]
    v = v_bf.astype(F32)

    tril = _tri(L, True)
    g = sum(_dot(tril, part) for part in _split3(lf))
    g_last = g[L - 1:L, :]
    q_in = (qa * jnp.exp(g)).astype(BF16)
    k_out = (kk * jnp.exp(g_last - g)).astype(BF16)
    s_dec = jnp.exp(g_last)

    row_i = lax.broadcasted_iota(jnp.int32, (L, L), 0)
    col_i = lax.broadcasted_iota(jnp.int32, (L, L), 1)

    levels = []
    hb = HGRN_BAND
    while hb < L:
        nb = L // (2 * hb)
        g3 = g.reshape(nb, 2 * hb, H_WIDTH)
        ge = jnp.broadcast_to(g3[:, hb - 1:hb, :], (nb, 2 * hb, H_WIDTH)).reshape(L, H_WIDTH)
        q_l = (qa * jnp.exp(jnp.minimum(g - ge, 0.0))).astype(BF16)
        k_l = (kk * jnp.exp(jnp.minimum(ge - g, 0.0))).astype(BF16)
        sh = (2 * hb).bit_length() - 1
        same_blk = jnp.right_shift(row_i, sh) == jnp.right_shift(col_i, sh)
        mask = same_blk & (jnp.bitwise_and(row_i, hb) != 0) & (jnp.bitwise_and(col_i, hb) == 0)
        levels.append((q_l, k_l, mask))
        hb *= 2

    sub_i = jnp.bitwise_and(lax.broadcasted_iota(jnp.int32, (L, HD), 0), HGRN_BAND - 1)
    band = []
    for d in range(HGRN_BAND):
        if d == 0:
            band.append((qa * kk, v))
        else:
            k_s = pltpu.roll(kk, d, axis=0)
            g_s = pltpu.roll(g, d, axis=0)
            v_s = pltpu.roll(v, d, axis=0)
            band.append((qa * k_s * jnp.exp(jnp.minimum(g - g_s, 0.0)), v_s))

    for h in range(H_HEADS):
        hs = slice(h * HD, (h + 1) * HD)
        st = st_sc[h]
        o = _dot_nt(q_in[:, hs], st.astype(BF16))
        a = jnp.zeros((L, L), F32)
        for q_l, k_l, mask in levels:
            a = a + jnp.where(mask, _dot_nt(q_l[:, hs], k_l[:, hs]), 0.0)
        o = o + _dot(a.astype(BF16), v_bf[:, hs])
        for d, (p_d, v_s) in enumerate(band):
            a_d = jnp.sum(p_d[:, hs], axis=1, keepdims=True)
            if d > 0:
                a_d = jnp.where(sub_i[:, 0:1] >= d, a_d, 0.0)
            o = o + a_d * v_s[:, hs]
        st_new = st * s_dec[:, hs] + _dot_tn(v_bf[:, hs], k_out[:, hs])
        st_sc[h] = st_new
        S_ref[h] = st_new.T
        y = o * lax.rsqrt(jnp.mean(o * o, axis=1, keepdims=True) + EPS) * hng_ref[:, hs]
        gz = ig_ref[:, H_WIDTH + h * HD:H_WIDTH + (h + 1) * HD].astype(F32)
        oh_ref[:, hs] = (y * _silu(gz)).astype(BF16)


def _hgrn(proj, hgrn_lb, hnorm_g, S0, B, T, L):
    NC = T // L
    has_state = S0 is not None
    off = 3 * M_INNER // (2 * H_WIDTH)
    const2 = lambda b, c: (0, 0)
    in_specs = [
        pl.BlockSpec((L, 2 * H_WIDTH), lambda b, c: (b * NC + c, off)),
        pl.BlockSpec((L, 2 * H_WIDTH), lambda b, c: (b * NC + c, off + 1)),
        pl.BlockSpec((2, H_WIDTH), const2),
        pl.BlockSpec((1, H_WIDTH), const2),
    ]
    args = [proj, proj, hgrn_lb, hnorm_g]
    s_spec = pl.BlockSpec((None, H_HEADS, H_EXPAND, H_EXPAND), lambda b, c: (b, 0, 0, 0))
    if has_state:
        in_specs.append(s_spec)
        args.append(S0)
    return pl.pallas_call(
        functools.partial(_hgrn_kernel, L=L, has_state=has_state),
        grid=(B, NC),
        in_specs=in_specs,
        out_specs=[pl.BlockSpec((L, H_WIDTH), lambda b, c: (b * NC + c, 0)), s_spec],
        out_shape=[
            jax.ShapeDtypeStruct((B * T, H_WIDTH), BF16),
            jax.ShapeDtypeStruct((B, H_HEADS, H_EXPAND, H_EXPAND), F32),
        ],
        scratch_shapes=[pltpu.VMEM((H_HEADS, H_EXPAND, H_EXPAND), F32)],
        compiler_params=pltpu.CompilerParams(
            dimension_semantics=("arbitrary", "arbitrary"), vmem_limit_bytes=VMEM_LIMIT),
        name="hgrn_state" if has_state else "hgrn",
    )(*args)


def _merge_kernel(x_ref, p_ref, hm_ref, oh_ref, gab_ref, wbrm_ref, wbrh_ref, wout_ref, wpg_ref, wple_ref,
                  fg_ref, o_ref):
    ya = _dot(hm_ref[...], wbrm_ref[...])
    yb = _dot(oh_ref[...], wbrh_ref[...])
    ga = gab_ref[:, 0:D_MODEL].astype(F32)
    gb = gab_ref[:, D_MODEL:2 * D_MODEL].astype(F32)
    y = _sigmoid(ga) * ya + _sigmoid(gb) * yb
    h1 = x_ref[...] + _dot(y.astype(BF16), wout_ref[...])
    pe = _dot(p_ref[...].astype(BF16), wple_ref[...])
    h2 = h1 + _sigmoid(_dot(h1.astype(BF16), wpg_ref[...])) * pe
    o_ref[...] = h2 * lax.rsqrt(jnp.mean(h2 * h2, axis=-1, keepdims=True) + EPS) * fg_ref[...]


def _merge(x2, p2, hm, oh, proj, w_brm, w_brh, w_out, w_pg, w_ple, final_g, tm):
    n_tok = x2.shape[0]
    gab_blk = (3 * M_INNER + 4 * H_WIDTH) // (2 * D_MODEL)
    tok = lambda i: (i, 0)
    const = lambda i: (0, 0)
    return pl.pallas_call(
        _merge_kernel,
        grid=(n_tok // tm,),
        in_specs=[
            pl.BlockSpec((tm, D_MODEL), tok),
            pl.BlockSpec((tm, PLE_DIM), tok),
            pl.BlockSpec((tm, M_INNER), tok),
            pl.BlockSpec((tm, H_WIDTH), tok),
            pl.BlockSpec((tm, 2 * D_MODEL), lambda i: (i, gab_blk)),
            pl.BlockSpec((M_INNER, D_MODEL), const),
            pl.BlockSpec((H_WIDTH, D_MODEL), const),
            pl.BlockSpec((D_MODEL, D_MODEL), const),
            pl.BlockSpec((D_MODEL, D_MODEL), const),
            pl.BlockSpec((PLE_DIM, D_MODEL), const),
            pl.BlockSpec((1, D_MODEL), const),
        ],
        out_specs=pl.BlockSpec((tm, D_MODEL), tok),
        out_shape=jax.ShapeDtypeStruct((n_tok, D_MODEL), F32),
        compiler_params=pltpu.CompilerParams(
            dimension_semantics=("arbitrary",), vmem_limit_bytes=VMEM_LIMIT),
        name="merge",
    )(x2, p2, hm, oh, proj, w_brm, w_brh, w_out, w_pg, w_ple, final_g)


def _trunk(x, p, state, w, L, tm):
    B, T, _ = x.shape
    x2 = x.reshape(B * T, D_MODEL)
    p2 = p.reshape(B * T, PLE_DIM)
    proj, gates = _proj(x2, w["norm_g"], w["w_main"], w["wg_hi"], w["wg_lo"], tm, 2 * D_MODEL)
    if state is None:
        m_state, s_state = None, None
    else:
        conv0, C0, n0, m0, S0 = state
        m0p = jnp.zeros((B, SUBLANES, GATE_LANES), F32).at[:, :M_HEADS, :].set(m0[:, :, None])
        m_state = (conv0, C0, n0.reshape(B, M_HEADS, 1, M_HD), m0p)
        s_state = S0
    hm, conv_n, C_n, n_n, m_n = _mlstm(proj, gates, w["gbias"], w["conv_w"], w["conv_b"], w["w_qm"], w["w_km"],
                                       w["mnorm_g"], w["m_skip"], m_state, B, T, L)
    oh, S_n = _hgrn(proj, w["hgrn_lb"], w["hnorm_g"], s_state, B, T, L)
    y = _merge(x2, p2, hm, oh, proj, w["w_brm"], w["w_brh"], w["w_out"], w["w_pg"], w["w_ple"], w["final_g"],
               tm // 2)
    return (y.reshape(B, T, D_MODEL), conv_n[None], C_n[None], n_n.reshape(B, M_HEADS, M_HD)[None],
            m_n[:, :M_HEADS, 0][None], S_n[None])


def kernel(x_prompt, x_sample, state_conv, state_mlstm_C, state_mlstm_n, state_mlstm_m, state_hgrn, p_prompt, p_sample, norm_g, w_in, b_ig, b_fg, conv_w, conv_b, w_qm, w_km, mnorm_g, m_skip, w_brm, hgrn_lb, hnorm_g, w_brh, w_out, w_ple, w_pg, final_g):
    assert norm_g.shape[0] == 1, "single-layer trunk"
    g0 = 3 * M_INNER
    g1 = g0 + 2 * M_HEADS
    w_in0 = w_in[0]
    w_gate = jnp.pad(w_in0[:, g0:g1], ((0, 0), (0, GATE_LANES - 2 * M_HEADS)))
    wg_hi = w_gate.astype(BF16)
    w = {
        "norm_g": norm_g[0][None, :],
        "w_main": jnp.concatenate([w_in0[:, :g0], w_in0[:, g1:]], axis=1).astype(BF16),
        "wg_hi": wg_hi,
        "wg_lo": (w_gate - wg_hi.astype(F32)).astype(BF16),
        "gbias": jnp.pad(jnp.concatenate([b_ig[0], b_fg[0]]), (0, GATE_LANES - 2 * M_HEADS))[None, :],
        "conv_w": conv_w[0],
        "conv_b": conv_b[0][None, :],
        "w_qm": w_qm[0].astype(BF16),
        "w_km": w_km[0].astype(BF16),
        "mnorm_g": mnorm_g[0][None, :],
        "m_skip": m_skip[0][None, :],
        "w_brm": w_brm[0].astype(BF16),
        "hgrn_lb": hgrn_lb,
        "hnorm_g": hnorm_g[0][None, :],
        "w_brh": w_brh[0].astype(BF16),
        "w_out": w_out[0].astype(BF16),
        "w_ple": w_ple[0].astype(BF16),
        "w_pg": w_pg[0].astype(BF16),
        "final_g": final_g[None, :],
    }
    T_s = x_sample.shape[1]
    out_p = _trunk(x_prompt, p_prompt[0], None, w, L=256, tm=1024)
    state = (state_conv[0], state_mlstm_C[0], state_mlstm_n[0], state_mlstm_m[0], state_hgrn[0])
    out_s = _trunk(x_sample, p_sample[0], state, w, L=T_s, tm=1024)
    return (out_p[0], out_s[0]) + tuple(out_p[1:]) + tuple(out_s[1:])
```

```python
import functools

import jax
import jax.numpy as jnp
from jax import lax
from jax.experimental import pallas as pl
from jax.experimental.pallas import tpu as pltpu

F32 = jnp.float32
BF16 = jnp.bfloat16

D_MODEL = 1024
PLE_DIM = 256
M_HEADS = 4
M_INNER = 2 * D_MODEL
M_HD = M_INNER // M_HEADS
CONV_W = 4
H_EXPAND = 128
H_WIDTH = D_MODEL
H_HEADS = H_WIDTH // H_EXPAND
EPS = 1e-6

N_MAIN = 3 * M_INNER + 4 * H_WIDTH + 2 * D_MODEL
GATE_LANES = 128
SUBLANES = 8
HGRN_BAND = SUBLANES
VMEM_LIMIT = 56 * 1024 * 1024


def _dot(a, b):
    return jnp.dot(a, b, preferred_element_type=F32)


def _dot_nt(a, b):
    return lax.dot_general(a, b, (((1,), (1,)), ((), ())), preferred_element_type=F32)


def _dot_tn(a, b):
    return lax.dot_general(a, b, (((0,), (0,)), ((), ())), preferred_element_type=F32)


def _split3(x):
    hi = x.astype(BF16)
    r = x - hi.astype(F32)
    mid = r.astype(BF16)
    lo = (r - mid.astype(F32)).astype(BF16)
    return hi, mid, lo


def _sigmoid(x):
    return 1.0 / (1.0 + jnp.exp(-x))


def _silu(x):
    return x * _sigmoid(x)


def _log_sigmoid(x):
    return jnp.minimum(x, 0.0) - jnp.log(1.0 + jnp.exp(-jnp.abs(x)))


def _tri(n, lower):
    r = lax.broadcasted_iota(jnp.int32, (n, n), 0)
    c = lax.broadcasted_iota(jnp.int32, (n, n), 1)
    return jnp.where((r >= c) if lower else (r <= c), 1.0, 0.0).astype(BF16)


def _proj_kernel(x_ref, g_ref, wa_ref, wb_ref, wgh_ref, wgl_ref, o_ref, gate_ref, xh_sc, xl_sc, *, n_a):
    j = pl.program_id(1)

    @pl.when(j == 0)
    def _():
        x = x_ref[...]
        xn = x * lax.rsqrt(jnp.mean(x * x, axis=-1, keepdims=True) + EPS) * g_ref[...]
        hi = xn.astype(BF16)
        lo = (xn - hi.astype(F32)).astype(BF16)
        xh_sc[...] = hi
        xl_sc[...] = lo
        gate_ref[...] = _dot(hi, wgh_ref[...]) + _dot(lo, wgh_ref[...]) + _dot(hi, wgl_ref[...])

    @pl.when(j < n_a)
    def _():
        o_ref[...] = _dot(xh_sc[...], wa_ref[...]).astype(BF16)

    @pl.when(j >= n_a)
    def _():
        o_ref[...] = _dot(xh_sc[...], wb_ref[...]).astype(BF16)


def _proj(x2, norm_g, w_a, w_b, wg_hi, wg_lo, tm, tn):
    n_tok = x2.shape[0]
    n_a = w_a.shape[1] // tn
    grid = (n_tok // tm, N_MAIN // tn)
    return pl.pallas_call(
        functools.partial(_proj_kernel, n_a=n_a),
        grid=grid,
        in_specs=[
            pl.BlockSpec((tm, D_MODEL), lambda i, j: (i, 0)),
            pl.BlockSpec((1, D_MODEL), lambda i, j: (0, 0)),
            pl.BlockSpec((D_MODEL, tn), lambda i, j: (0, jnp.minimum(j, n_a - 1))),
            pl.BlockSpec((D_MODEL, tn), lambda i, j: (0, jnp.maximum(j - n_a, 0))),
            pl.BlockSpec((D_MODEL, GATE_LANES), lambda i, j: (0, 0)),
            pl.BlockSpec((D_MODEL, GATE_LANES), lambda i, j: (0, 0)),
        ],
        out_specs=[
            pl.BlockSpec((tm, tn), lambda i, j: (i, j)),
            pl.BlockSpec((tm, GATE_LANES), lambda i, j: (i, 0)),
        ],
        out_shape=[
            jax.ShapeDtypeStruct((n_tok, N_MAIN), BF16),
            jax.ShapeDtypeStruct((n_tok, GATE_LANES), F32),
        ],
        scratch_shapes=[pltpu.VMEM((tm, D_MODEL), BF16), pltpu.VMEM((tm, D_MODEL), BF16)],
        compiler_params=pltpu.CompilerParams(
            dimension_semantics=("arbitrary", "arbitrary"), vmem_limit_bytes=VMEM_LIMIT),
        name="proj",
    )(x2, norm_g, w_a, w_b, wg_hi, wg_lo)


def _mlstm_kernel(*refs, L, has_state):
    (pm_ref, gate_ref, gbias_ref, convw_ref, convb_ref, wq_ref, wk_ref, mng_ref, skip_ref) = refs[:9]
    if has_state:
        conv0_ref, C0_ref, n0_ref, m0_ref = refs[9:13]
        rest = refs[13:]
    else:
        rest = refs[9:]
    hm_ref, convo_ref, C_ref, n_ref, m_ref, uext_sc = rest
    c_idx = pl.program_id(1)
    TAIL = CONV_W - 1
    BASE = SUBLANES

    @pl.when(c_idx == 0)
    def _():
        if has_state:
            uext_sc[BASE - TAIL:BASE, :] = conv0_ref[...]
            C_ref[...] = C0_ref[...]
            n_ref[...] = n0_ref[...]
            m_ref[...] = m0_ref[...]
        else:
            uext_sc[BASE - TAIL:BASE, :] = jnp.zeros((TAIL, M_INNER), F32)
            C_ref[...] = jnp.zeros(C_ref.shape, F32)
            n_ref[...] = jnp.zeros(n_ref.shape, F32)
            m_ref[...] = jnp.zeros(m_ref.shape, F32)

    uext_sc[BASE:BASE + L, :] = pm_ref[:, 0:M_INNER].astype(F32)
    conv = convb_ref[...] + uext_sc[BASE - TAIL:BASE - TAIL + L, :] * convw_ref[0:1, :]
    for j in range(1, CONV_W):
        conv = conv + uext_sc[BASE - TAIL + j:BASE - TAIL + j + L, :] * convw_ref[j:j + 1, :]
    tail = uext_sc[BASE + L - TAIL:BASE + L, :]
    uext_sc[BASE - TAIL:BASE, :] = tail
    convo_ref[...] = tail
    c_act = _silu(conv)

    gates = gate_ref[...] + gbias_ref[...]
    lf_cols = _log_sigmoid(gates)
    gates_t = gates.T[0:2 * M_HEADS, :]
    lf_rows = _log_sigmoid(gates_t)
    tril, triu = _tri(L, True), _tri(L, False)
    b_cols = sum(_dot(tril, part) for part in _split3(lf_cols))
    b_rows = sum(_dot(part, triu) for part in _split3(lf_rows))
    row_i = lax.broadcasted_iota(jnp.int32, (L, L), 0)
    col_i = lax.broadcasted_iota(jnp.int32, (L, L), 1)
    causal = row_i >= col_i

    for h in range(M_HEADS):
        hs = slice(h * M_HD, (h + 1) * M_HD)
        ch = c_act[:, hs].astype(BF16)
        q = _dot(ch, wq_ref[h])
        k = _dot(ch, wk_ref[h]) * (M_HD ** -0.5)
        v = pm_ref[:, M_INNER + h * M_HD:M_INNER + (h + 1) * M_HD]
        q_bf = q.astype(BF16)
        k_bf = k.astype(BF16)

        ig_col = gates[:, h:h + 1]
        ig_row = gates_t[h:h + 1, :]
        b_col = b_cols[:, M_HEADS + h:M_HEADS + h + 1]
        b_row = b_rows[M_HEADS + h:M_HEADS + h + 1, :]
        m_c = m_ref[h:h + 1, 0:1]

        logd = jnp.where(causal, b_col - b_row + ig_row, -jnp.inf)
        m_prev = b_col + m_c
        m_t = jnp.maximum(m_prev, jnp.max(logd, axis=1, keepdims=True))
        dmat = jnp.exp(logd - m_t)
        w_inter = jnp.exp(m_prev - m_t)

        C = C_ref[h]
        n = n_ref[h]
        s_mat = _dot_nt(q_bf, k_bf) * dmat
        num = w_inter * _dot_nt(q_bf, C.astype(BF16)) + _dot(s_mat.astype(BF16), v)
        den = w_inter * jnp.sum(q * n, axis=1, keepdims=True) + jnp.sum(s_mat, axis=1, keepdims=True)
        hh = num * (1.0 / jnp.maximum(jnp.abs(den), jnp.exp(-m_t)))

        m_last = m_t[L - 1:L, :]
        w_last = jnp.exp(b_col[L - 1:L, :] - b_col + ig_col - m_last)
        decay = jnp.exp(m_prev[L - 1:L, :] - m_last)
        wv = (w_last * v.astype(F32)).astype(BF16)
        C_ref[h] = decay * C + _dot_tn(wv, k_bf)
        n_ref[h] = decay * n + jnp.sum(w_last * k, axis=0, keepdims=True)
        m_ref[h:h + 1, :] = jnp.broadcast_to(m_last, (1, m_ref.shape[1]))

        mu = jnp.mean(hh, axis=1, keepdims=True)
        d = hh - mu
        y = d * lax.rsqrt(jnp.mean(d * d, axis=1, keepdims=True) + EPS) * mng_ref[:, hs]
        z = pm_ref[:, 2 * M_INNER + h * M_HD:2 * M_INNER + (h + 1) * M_HD].astype(F32)
        hm_ref[:, hs] = ((y + skip_ref[:, hs] * c_act[:, hs]) * _silu(z)).astype(BF16)


def _mlstm(proj, gates, gbias, conv_w, conv_b, wq, wk, mnorm_g, m_skip, state, B, T, L):
    NC = T // L
    has_state = state is not None
    tok = lambda b, c: (b * NC + c, 0)
    const2 = lambda b, c: (0, 0)
    in_specs = [
        pl.BlockSpec((L, 3 * M_INNER), tok),
        pl.BlockSpec((L, GATE_LANES), tok),
        pl.BlockSpec((1, GATE_LANES), const2),
        pl.BlockSpec((CONV_W, M_INNER), const2),
        pl.BlockSpec((1, M_INNER), const2),
        pl.BlockSpec((M_HEADS, M_HD, M_HD), lambda b, c: (0, 0, 0)),
        pl.BlockSpec((M_HEADS, M_HD, M_HD), lambda b, c: (0, 0, 0)),
        pl.BlockSpec((1, M_INNER), const2),
        pl.BlockSpec((1, M_INNER), const2),
    ]
    args = [proj, gates, gbias, conv_w, conv_b, wq, wk, mnorm_g, m_skip]
    state_specs = [
        pl.BlockSpec((None, CONV_W - 1, M_INNER), lambda b, c: (b, 0, 0)),
        pl.BlockSpec((None, M_HEADS, M_HD, M_HD), lambda b, c: (b, 0, 0, 0)),
        pl.BlockSpec((None, M_HEADS, 1, M_HD), lambda b, c: (b, 0, 0, 0)),
        pl.BlockSpec((None, SUBLANES, GATE_LANES), lambda b, c: (b, 0, 0)),
    ]
    if has_state:
        in_specs += state_specs
        args += list(state)
    return pl.pallas_call(
        functools.partial(_mlstm_kernel, L=L, has_state=has_state),
        grid=(B, NC),
        in_specs=in_specs,
        out_specs=[pl.BlockSpec((L, M_INNER), tok)] + state_specs,
        out_shape=[
            jax.ShapeDtypeStruct((B * T, M_INNER), BF16),
            jax.ShapeDtypeStruct((B, CONV_W - 1, M_INNER), F32),
            jax.ShapeDtypeStruct((B, M_HEADS, M_HD, M_HD), F32),
            jax.ShapeDtypeStruct((B, M_HEADS, 1, M_HD), F32),
            jax.ShapeDtypeStruct((B, SUBLANES, GATE_LANES), F32),
        ],
        scratch_shapes=[pltpu.VMEM((L + 2 * SUBLANES, M_INNER), F32)],
        compiler_params=pltpu.CompilerParams(
            dimension_semantics=("arbitrary", "arbitrary"), vmem_limit_bytes=VMEM_LIMIT),
        name="mlstm_state" if has_state else "mlstm",
    )(*args)


def _hgrn_kernel(*refs, L, has_state):
    qf_ref, ig_ref, lb_ref, hng_ref = refs[:4]
    if has_state:
        S0_ref = refs[4]
        rest = refs[5:]
    else:
        rest = refs[4:]
    oh_ref, S_ref, st_sc = rest
    c_idx = pl.program_id(1)
    HD = H_EXPAND

    @pl.when(c_idx == 0)
    def _():
        for h in range(H_HEADS):
            if has_state:
                st_sc[h] = S0_ref[h].T
            else:
                st_sc[h] = jnp.zeros((HD, HD), F32)

    lbp = lb_ref[...]
    lbm = jnp.max(lbp, axis=0, keepdims=True)
    lbe = jnp.exp(lbp - lbm)
    lb = lbe[0:1, :] / jnp.sum(lbe, axis=0, keepdims=True)

    f = qf_ref[:, H_WIDTH:2 * H_WIDTH].astype(F32)
    e = jnp.exp(-jnp.abs(f))
    r = 1.0 / (1.0 + e)
    sig_pos = jnp.where(f >= 0, r, e * r)
    sig_neg = jnp.where(f >= 0, e * r, r)
    fgt = lb + (1.0 - lb) * sig_pos
    lf = jnp.log(fgt)
    kk = (1.0 - lb) * sig_neg
    qa = _silu(qf_ref[:, 0:H_WIDTH].astype(F32))
    v_bf = ig_ref[:, 0:H_WIDTH]
    v = v_bf.astype(F32)

    tril = _tri(L, True)
    g = sum(_dot(tril, part) for part in _split3(lf))
    g_last = g[L - 1:L, :]
    q_in = (qa * jnp.exp(g)).astype(BF16)
    k_out = (kk * jnp.exp(g_last - g)).astype(BF16)
    s_dec = jnp.exp(g_last)

    row_i = lax.broadcasted_iota(jnp.int32, (L, L), 0)
    col_i = lax.broadcasted_iota(jnp.int32, (L, L), 1)

    levels = []
    hb = HGRN_BAND
    while hb < L:
        nb = L // (2 * hb)
        g3 = g.reshape(nb, 2 * hb, H_WIDTH)
        ge = jnp.broadcast_to(g3[:, hb - 1:hb, :], (nb, 2 * hb, H_WIDTH)).reshape(L, H_WIDTH)
        e_l = jnp.exp(-jnp.abs(g - ge))
        q_l = (qa * e_l).astype(BF16)
        k_l = (kk * e_l).astype(BF16)
        sh = (2 * hb).bit_length() - 1
        same_blk = jnp.right_shift(row_i, sh) == jnp.right_shift(col_i, sh)
        mask = same_blk & (jnp.bitwise_and(row_i, hb) != 0) & (jnp.bitwise_and(col_i, hb) == 0)
        levels.append((q_l, k_l, mask))
        hb *= 2

    nblk = L // HGRN_BAND
    to_blocks = lambda a: a.reshape(nblk, HGRN_BAND, H_WIDTH)
    sub_i = lax.broadcasted_iota(jnp.int32, (nblk, HGRN_BAND, H_WIDTH), 1)
    f0 = jnp.where(sub_i == 0, 0.0, to_blocks(fgt))
    qa3 = to_blocks(qa)
    ke = to_blocks(kk)
    vs = to_blocks(v)
    band = [(qa3 * ke, vs)]
    for d in range(1, HGRN_BAND):
        ke = f0 * pltpu.roll(ke, 1, axis=1)
        vs = pltpu.roll(vs, 1, axis=1)
        band.append((qa3 * ke, vs))

    for h in range(H_HEADS):
        hs = slice(h * HD, (h + 1) * HD)
        st = st_sc[h]
        o = _dot_nt(q_in[:, hs], st.astype(BF16))
        a = jnp.zeros((L, L), F32)
        for q_l, k_l, mask in levels:
            a = a + jnp.where(mask, _dot_nt(q_l[:, hs], k_l[:, hs]), 0.0)
        o = o + _dot(a.astype(BF16), v_bf[:, hs])
        o_band = None
        for p_d, v_d in band:
            term = jnp.sum(p_d[:, :, hs], axis=2, keepdims=True) * v_d[:, :, hs]
            o_band = term if o_band is None else o_band + term
        o = o + o_band.reshape(L, HD)
        st_new = st * s_dec[:, hs] + _dot_tn(v_bf[:, hs], k_out[:, hs])
        st_sc[h] = st_new
        y = o * lax.rsqrt(jnp.mean(o * o, axis=1, keepdims=True) + EPS) * hng_ref[:, hs]
        gz = ig_ref[:, H_WIDTH + h * HD:H_WIDTH + (h + 1) * HD].astype(F32)
        oh_ref[:, hs] = (y * _silu(gz)).astype(BF16)

    @pl.when(c_idx == pl.num_programs(1) - 1)
    def _():
        for h in range(H_HEADS):
            S_ref[h] = st_sc[h].T


def _hgrn(proj, hgrn_lb, hnorm_g, S0, B, T, L):
    NC = T // L
    has_state = S0 is not None
    off = 3 * M_INNER // (2 * H_WIDTH)
    const2 = lambda b, c: (0, 0)
    in_specs = [
        pl.BlockSpec((L, 2 * H_WIDTH), lambda b, c: (b * NC + c, off)),
        pl.BlockSpec((L, 2 * H_WIDTH), lambda b, c: (b * NC + c, off + 1)),
        pl.BlockSpec((2, H_WIDTH), const2),
        pl.BlockSpec((1, H_WIDTH), const2),
    ]
    args = [proj, proj, hgrn_lb, hnorm_g]
    s_spec = pl.BlockSpec((None, H_HEADS, H_EXPAND, H_EXPAND), lambda b, c: (b, 0, 0, 0))
    if has_state:
        in_specs.append(s_spec)
        args.append(S0)
    return pl.pallas_call(
        functools.partial(_hgrn_kernel, L=L, has_state=has_state),
        grid=(B, NC),
        in_specs=in_specs,
        out_specs=[pl.BlockSpec((L, H_WIDTH), lambda b, c: (b * NC + c, 0)), s_spec],
        out_shape=[
            jax.ShapeDtypeStruct((B * T, H_WIDTH), BF16),
            jax.ShapeDtypeStruct((B, H_HEADS, H_EXPAND, H_EXPAND), F32),
        ],
        scratch_shapes=[pltpu.VMEM((H_HEADS, H_EXPAND, H_EXPAND), F32)],
        compiler_params=pltpu.CompilerParams(
            dimension_semantics=("arbitrary", "arbitrary"), vmem_limit_bytes=VMEM_LIMIT),
        name="hgrn_state" if has_state else "hgrn",
    )(*args)


def _merge_kernel(x_ref, p_ref, hm_ref, oh_ref, gab_ref, wbrm_ref, wbrh_ref, wout_ref, wpg_ref, wple_ref,
                  fg_ref, o_ref):
    ya = _dot(hm_ref[...], wbrm_ref[...])
    yb = _dot(oh_ref[...], wbrh_ref[...])
    ga = gab_ref[:, 0:D_MODEL].astype(F32)
    gb = gab_ref[:, D_MODEL:2 * D_MODEL].astype(F32)
    y = _sigmoid(ga) * ya + _sigmoid(gb) * yb
    h1 = x_ref[...] + _dot(y.astype(BF16), wout_ref[...])
    pe = _dot(p_ref[...].astype(BF16), wple_ref[...])
    h2 = h1 + _sigmoid(_dot(h1.astype(BF16), wpg_ref[...])) * pe
    o_ref[...] = h2 * lax.rsqrt(jnp.mean(h2 * h2, axis=-1, keepdims=True) + EPS) * fg_ref[...]


def _merge(x2, p2, hm, oh, proj, w_brm, w_brh, w_out, w_pg, w_ple, final_g, tm):
    n_tok = x2.shape[0]
    gab_blk = (3 * M_INNER + 4 * H_WIDTH) // (2 * D_MODEL)
    tok = lambda i: (i, 0)
    const = lambda i: (0, 0)
    return pl.pallas_call(
        _merge_kernel,
        grid=(n_tok // tm,),
        in_specs=[
            pl.BlockSpec((tm, D_MODEL), tok),
            pl.BlockSpec((tm, PLE_DIM), tok),
            pl.BlockSpec((tm, M_INNER), tok),
            pl.BlockSpec((tm, H_WIDTH), tok),
            pl.BlockSpec((tm, 2 * D_MODEL), lambda i: (i, gab_blk)),
            pl.BlockSpec((M_INNER, D_MODEL), const),
            pl.BlockSpec((H_WIDTH, D_MODEL), const),
            pl.BlockSpec((D_MODEL, D_MODEL), const),
            pl.BlockSpec((D_MODEL, D_MODEL), const),
            pl.BlockSpec((PLE_DIM, D_MODEL), const),
            pl.BlockSpec((1, D_MODEL), const),
        ],
        out_specs=pl.BlockSpec((tm, D_MODEL), tok),
        out_shape=jax.ShapeDtypeStruct((n_tok, D_MODEL), F32),
        compiler_params=pltpu.CompilerParams(
            dimension_semantics=("arbitrary",), vmem_limit_bytes=VMEM_LIMIT),
        name="merge",
    )(x2, p2, hm, oh, proj, w_brm, w_brh, w_out, w_pg, w_ple, final_g)


def _trunk(x, p, state, w, L, tm):
    B, T, _ = x.shape
    x2 = x.reshape(B * T, D_MODEL)
    p2 = p.reshape(B * T, PLE_DIM)
    proj, gates = _proj(x2, w["norm_g"], w["w_a"], w["w_b"], w["wg_hi"], w["wg_lo"], tm, 2 * D_MODEL)
    if state is None:
        m_state, s_state = None, None
    else:
        conv0, C0, n0, m0, S0 = state
        m0p = jnp.zeros((B, SUBLANES, GATE_LANES), F32).at[:, :M_HEADS, :].set(m0[:, :, None])
        m_state = (conv0, C0, n0.reshape(B, M_HEADS, 1, M_HD), m0p)
        s_state = S0
    hm, conv_n, C_n, n_n, m_n = _mlstm(proj, gates, w["gbias"], w["conv_w"], w["conv_b"], w["w_qm"], w["w_km"],
                                       w["mnorm_g"], w["m_skip"], m_state, B, T, L)
    oh, S_n = _hgrn(proj, w["hgrn_lb"], w["hnorm_g"], s_state, B, T, L)
    y = _merge(x2, p2, hm, oh, proj, w["w_brm"], w["w_brh"], w["w_out"], w["w_pg"], w["w_ple"], w["final_g"],
               tm // 2)
    return (y.reshape(B, T, D_MODEL), conv_n[None], C_n[None], n_n.reshape(B, M_HEADS, M_HD)[None],
            m_n[:, :M_HEADS, 0][None], S_n[None])


def _prep_weights(norm_g, w_in, b_ig, b_fg, conv_w, conv_b, w_qm, w_km, mnorm_g, m_skip, w_brm, hgrn_lb, hnorm_g,
                  w_brh, w_out, w_ple, w_pg, final_g):
    assert norm_g.shape[0] == 1, "single-layer trunk"
    g0 = 3 * M_INNER
    g1 = g0 + 2 * M_HEADS
    w_in0 = w_in[0]
    w_gate = w_in0[:, g0:g0 + GATE_LANES]
    wg_hi = w_gate.astype(BF16)
    return {
        "norm_g": norm_g[0][None, :],
        "w_a": w_in0[:, :g0].astype(BF16),
        "w_b": w_in0[:, g1:].astype(BF16),
        "wg_hi": wg_hi,
        "wg_lo": (w_gate - wg_hi.astype(F32)).astype(BF16),
        "gbias": jnp.pad(jnp.concatenate([b_ig[0], b_fg[0]]), (0, GATE_LANES - 2 * M_HEADS))[None, :],
        "conv_w": conv_w[0],
        "conv_b": conv_b[0][None, :],
        "w_qm": w_qm[0].astype(BF16),
        "w_km": w_km[0].astype(BF16),
        "mnorm_g": mnorm_g[0][None, :],
        "m_skip": m_skip[0][None, :],
        "w_brm": w_brm[0].astype(BF16),
        "hgrn_lb": hgrn_lb,
        "hnorm_g": hnorm_g[0][None, :],
        "w_brh": w_brh[0].astype(BF16),
        "w_out": w_out[0].astype(BF16),
        "w_ple": w_ple[0].astype(BF16),
        "w_pg": w_pg[0].astype(BF16),
        "final_g": final_g[None, :],
    }


def kernel(x_prompt, x_sample, state_conv, state_mlstm_C, state_mlstm_n, state_mlstm_m, state_hgrn, p_prompt, p_sample, norm_g, w_in, b_ig, b_fg, conv_w, conv_b, w_qm, w_km, mnorm_g, m_skip, w_brm, hgrn_lb, hnorm_g, w_brh, w_out, w_ple, w_pg, final_g):
    w = _prep_weights(norm_g, w_in, b_ig, b_fg, conv_w, conv_b, w_qm, w_km, mnorm_g, m_skip, w_brm, hgrn_lb,
                      hnorm_g, w_brh, w_out, w_ple, w_pg, final_g)
    T_s = x_sample.shape[1]
    out_p = _trunk(x_prompt, p_prompt[0], None, w, L=256, tm=1024)
    state = (state_conv[0], state_mlstm_C[0], state_mlstm_n[0], state_mlstm_m[0], state_hgrn[0])
    out_s = _trunk(x_sample, p_sample[0], state, w, L=T_s, tm=1024)
    return (out_p[0], out_s[0]) + tuple(out_p[1:]) + tuple(out_s[1:])
```

```python
import functools

import jax
import jax.numpy as jnp
from jax import lax
from jax.experimental import pallas as pl
from jax.experimental.pallas import tpu as pltpu

F32 = jnp.float32
BF16 = jnp.bfloat16

D_MODEL = 1024
PLE_DIM = 256
M_HEADS = 4
M_INNER = 2 * D_MODEL
M_HD = M_INNER // M_HEADS
CONV_W = 4
H_EXPAND = 128
H_WIDTH = D_MODEL
H_HEADS = H_WIDTH // H_EXPAND
EPS = 1e-6

N_A = 3 * M_INNER
N_B = 4 * H_WIDTH + 2 * D_MODEL
GATE_LANES = 128
SUBLANES = 8
HGRN_BAND = SUBLANES
VMEM_LIMIT = 56 * 1024 * 1024


def _dot(a, b):
    return jnp.dot(a, b, preferred_element_type=F32)


def _dot_nt(a, b):
    return lax.dot_general(a, b, (((1,), (1,)), ((), ())), preferred_element_type=F32)


def _dot_tn(a, b):
    return lax.dot_general(a, b, (((0,), (0,)), ((), ())), preferred_element_type=F32)


def _split3(x):
    hi = x.astype(BF16)
    r = x - hi.astype(F32)
    mid = r.astype(BF16)
    lo = (r - mid.astype(F32)).astype(BF16)
    return hi, mid, lo


def _sigmoid(x):
    return 1.0 / (1.0 + jnp.exp(-x))


def _silu(x):
    return x * _sigmoid(x)


def _log_sigmoid(x):
    return jnp.minimum(x, 0.0) - jnp.log(1.0 + jnp.exp(-jnp.abs(x)))


def _tri(n, lower):
    r = lax.broadcasted_iota(jnp.int32, (n, n), 0)
    c = lax.broadcasted_iota(jnp.int32, (n, n), 1)
    return jnp.where((r >= c) if lower else (r <= c), 1.0, 0.0).astype(BF16)


def _rmsnorm_split(x_ref, g_ref):
    x = x_ref[...]
    xn = x * lax.rsqrt(jnp.mean(x * x, axis=-1, keepdims=True) + EPS) * g_ref[...]
    hi = xn.astype(BF16)
    lo = (xn - hi.astype(F32)).astype(BF16)
    return hi, lo


def _gate_dot(hi, lo, wgh_ref, wgl_ref):
    return _dot(hi, wgh_ref[...]) + _dot(lo, wgh_ref[...]) + _dot(hi, wgl_ref[...])


def _resident(shape):
    return pl.BlockSpec(shape, lambda *_: (0,) * len(shape), pipeline_mode=pl.Buffered(1))


def _proj_kernel(x_ref, g_ref, wa_ref, wb_ref, wgh_ref, wgl_ref, o_ref, gate_ref, xh_sc, *, n_a):
    j = pl.program_id(1)

    @pl.when(j == 0)
    def _():
        hi, lo = _rmsnorm_split(x_ref, g_ref)
        xh_sc[...] = hi
        gate_ref[...] = _gate_dot(hi, lo, wgh_ref, wgl_ref)

    @pl.when(j < n_a)
    def _():
        o_ref[...] = _dot(xh_sc[...], wa_ref[...]).astype(BF16)

    @pl.when(j >= n_a)
    def _():
        o_ref[...] = _dot(xh_sc[...], wb_ref[...]).astype(BF16)


def _proj(x2, norm_g, w_a, w_b, wg_hi, wg_lo, tm, tn):
    n_tok = x2.shape[0]
    n_a = N_A // tn
    return pl.pallas_call(
        functools.partial(_proj_kernel, n_a=n_a),
        grid=(n_tok // tm, (N_A + N_B) // tn),
        in_specs=[
            pl.BlockSpec((tm, D_MODEL), lambda i, j: (i, 0)),
            pl.BlockSpec((1, D_MODEL), lambda i, j: (0, 0)),
            pl.BlockSpec((D_MODEL, tn), lambda i, j: (0, jnp.minimum(j, n_a - 1))),
            pl.BlockSpec((D_MODEL, tn), lambda i, j: (0, jnp.maximum(j - n_a, 0))),
            pl.BlockSpec((D_MODEL, GATE_LANES), lambda i, j: (0, 0)),
            pl.BlockSpec((D_MODEL, GATE_LANES), lambda i, j: (0, 0)),
        ],
        out_specs=[
            pl.BlockSpec((tm, tn), lambda i, j: (i, j)),
            pl.BlockSpec((tm, GATE_LANES), lambda i, j: (i, 0)),
        ],
        out_shape=[
            jax.ShapeDtypeStruct((n_tok, N_A + N_B), BF16),
            jax.ShapeDtypeStruct((n_tok, GATE_LANES), F32),
        ],
        scratch_shapes=[pltpu.VMEM((tm, D_MODEL), BF16)],
        compiler_params=pltpu.CompilerParams(
            dimension_semantics=("arbitrary", "arbitrary"), vmem_limit_bytes=VMEM_LIMIT),
        name="proj",
    )(x2, norm_g, w_a, w_b, wg_hi, wg_lo)


def _mlstm_kernel(*refs, L, has_state):
    if has_state:
        (pm_ref, gate_ref, gbias_ref, convw_ref, convb_ref, wq_ref, wk_ref, mng_ref, skip_ref,
         conv0_ref, C0_ref, n0_ref, m0_ref) = refs[:13]
        rest = refs[13:]
    else:
        (x_ref, ng_ref, wa_ref, wgh_ref, wgl_ref, gbias_ref, convw_ref, convb_ref, wq_ref, wk_ref, mng_ref,
         skip_ref) = refs[:12]
        rest = refs[12:]
    hm_ref, convo_ref, C_ref, n_ref, m_ref, uext_sc = rest
    c_idx = pl.program_id(1)
    TAIL = CONV_W - 1
    BASE = SUBLANES

    @pl.when(c_idx == 0)
    def _():
        if has_state:
            uext_sc[BASE - TAIL:BASE, :] = conv0_ref[...]
            C_ref[...] = C0_ref[...]
            n_ref[...] = n0_ref[...]
            m_ref[...] = m0_ref[...]
        else:
            uext_sc[BASE - TAIL:BASE, :] = jnp.zeros((TAIL, M_INNER), F32)
            C_ref[...] = jnp.zeros(C_ref.shape, F32)
            n_ref[...] = jnp.zeros(n_ref.shape, F32)
            m_ref[...] = jnp.zeros(m_ref.shape, F32)

    if has_state:
        u = pm_ref[:, 0:M_INNER].astype(F32)
        gates = gate_ref[...] + gbias_ref[...]
        v_of = lambda h: pm_ref[:, M_INNER + h * M_HD:M_INNER + (h + 1) * M_HD].astype(F32)
        z_of = lambda h: pm_ref[:, 2 * M_INNER + h * M_HD:2 * M_INNER + (h + 1) * M_HD].astype(F32)
    else:
        xh, xl = _rmsnorm_split(x_ref, ng_ref)
        u = _dot(xh, wa_ref[:, 0:M_INNER])
        gates = _gate_dot(xh, xl, wgh_ref, wgl_ref) + gbias_ref[...]
        v_of = lambda h: _dot(xh, wa_ref[:, M_INNER + h * M_HD:M_INNER + (h + 1) * M_HD])
        z_of = lambda h: _dot(xh, wa_ref[:, 2 * M_INNER + h * M_HD:2 * M_INNER + (h + 1) * M_HD])

    uext_sc[BASE:BASE + L, :] = u
    conv = convb_ref[...] + uext_sc[BASE - TAIL:BASE - TAIL + L, :] * convw_ref[0:1, :]
    for j in range(1, CONV_W):
        conv = conv + uext_sc[BASE - TAIL + j:BASE - TAIL + j + L, :] * convw_ref[j:j + 1, :]
    tail = uext_sc[BASE + L - TAIL:BASE + L, :]
    uext_sc[BASE - TAIL:BASE, :] = tail
    convo_ref[...] = tail
    c_act = _silu(conv)

    lf_cols = _log_sigmoid(gates)
    gates_t = gates.T[0:2 * M_HEADS, :]
    lf_rows = _log_sigmoid(gates_t)
    tril, triu = _tri(L, True), _tri(L, False)
    b_cols = sum(_dot(tril, part) for part in _split3(lf_cols))
    b_rows = sum(_dot(part, triu) for part in _split3(lf_rows))
    row_i = lax.broadcasted_iota(jnp.int32, (L, L), 0)
    col_i = lax.broadcasted_iota(jnp.int32, (L, L), 1)
    causal = row_i >= col_i

    for h in range(M_HEADS):
        hs = slice(h * M_HD, (h + 1) * M_HD)
        ch = c_act[:, hs].astype(BF16)
        q = _dot(ch, wq_ref[h])
        k = _dot(ch, wk_ref[h]) * (M_HD ** -0.5)
        v = v_of(h)
        q_bf = q.astype(BF16)
        k_bf = k.astype(BF16)

        ig_col = gates[:, h:h + 1]
        ig_row = gates_t[h:h + 1, :]
        b_col = b_cols[:, M_HEADS + h:M_HEADS + h + 1]
        b_row = b_rows[M_HEADS + h:M_HEADS + h + 1, :]
        m_c = m_ref[h:h + 1, 0:1]

        logd = jnp.where(causal, b_col - b_row + ig_row, -jnp.inf)
        m_prev = b_col + m_c
        m_t = jnp.maximum(m_prev, jnp.max(logd, axis=1, keepdims=True))
        dmat = jnp.exp(logd - m_t)
        w_inter = jnp.exp(m_prev - m_t)

        C = C_ref[h]
        n = n_ref[h]
        s_mat = _dot_nt(q_bf, k_bf) * dmat
        num = w_inter * _dot_nt(q_bf, C.astype(BF16)) + _dot(s_mat.astype(BF16), v.astype(BF16))
        den = w_inter * jnp.sum(q * n, axis=1, keepdims=True) + jnp.sum(s_mat, axis=1, keepdims=True)
        hh = num * (1.0 / jnp.maximum(jnp.abs(den), jnp.exp(-m_t)))

        m_last = m_t[L - 1:L, :]
        w_last = jnp.exp(b_col[L - 1:L, :] - b_col + ig_col - m_last)
        decay = jnp.exp(m_prev[L - 1:L, :] - m_last)
        wv = (w_last * v).astype(BF16)
        C_ref[h] = decay * C + _dot_tn(wv, k_bf)
        n_ref[h] = decay * n + jnp.sum(w_last * k, axis=0, keepdims=True)
        m_ref[h:h + 1, :] = jnp.broadcast_to(m_last, (1, m_ref.shape[1]))

        mu = jnp.mean(hh, axis=1, keepdims=True)
        d = hh - mu
        y = d * lax.rsqrt(jnp.mean(d * d, axis=1, keepdims=True) + EPS) * mng_ref[:, hs]
        hm_ref[:, hs] = ((y + skip_ref[:, hs] * c_act[:, hs]) * _silu(z_of(h))).astype(BF16)


def _mlstm(src, w, state, B, T, L):
    NC = T // L
    has_state = state is not None
    tok = lambda b, c: (b * NC + c, 0)
    shared = [w["gbias"], w["conv_w"], w["conv_b"], w["w_qm"], w["w_km"], w["mnorm_g"], w["m_skip"]]
    shared_specs = [_resident(a.shape) for a in shared]
    state_specs = [
        pl.BlockSpec((None, CONV_W - 1, M_INNER), lambda b, c: (b, 0, 0)),
        pl.BlockSpec((None, M_HEADS, M_HD, M_HD), lambda b, c: (b, 0, 0, 0)),
        pl.BlockSpec((None, M_HEADS, 1, M_HD), lambda b, c: (b, 0, 0, 0)),
        pl.BlockSpec((None, SUBLANES, GATE_LANES), lambda b, c: (b, 0, 0)),
    ]
    if has_state:
        proj, gates = src
        args = [proj, gates] + shared + list(state)
        in_specs = [pl.BlockSpec((L, N_A), tok), pl.BlockSpec((L, GATE_LANES), tok)] + shared_specs + state_specs
    else:
        front = [w["norm_g"], w["w_a"], w["wg_hi"], w["wg_lo"]]
        args = [src] + front + shared
        in_specs = [pl.BlockSpec((L, D_MODEL), tok)] + [_resident(a.shape) for a in front] + shared_specs
    return pl.pallas_call(
        functools.partial(_mlstm_kernel, L=L, has_state=has_state),
        grid=(B, NC),
        in_specs=in_specs,
        out_specs=[pl.BlockSpec((L, M_INNER), tok)] + state_specs,
        out_shape=[
            jax.ShapeDtypeStruct((B * T, M_INNER), BF16),
            jax.ShapeDtypeStruct((B, CONV_W - 1, M_INNER), F32),
            jax.ShapeDtypeStruct((B, M_HEADS, M_HD, M_HD), F32),
            jax.ShapeDtypeStruct((B, M_HEADS, 1, M_HD), F32),
            jax.ShapeDtypeStruct((B, SUBLANES, GATE_LANES), F32),
        ],
        scratch_shapes=[pltpu.VMEM((L + 2 * SUBLANES, M_INNER), F32)],
        compiler_params=pltpu.CompilerParams(
            dimension_semantics=("arbitrary", "arbitrary"), vmem_limit_bytes=VMEM_LIMIT),
        name="mlstm_state" if has_state else "mlstm",
    )(*args)


def _hgrn_kernel(*refs, L, has_state):
    if has_state:
        qf_ref, ig_ref, lb_ref, hng_ref, S0_ref = refs[:5]
        oh_ref, S_ref, st_sc = refs[5:]
    else:
        x_ref, ng_ref, wb_ref, lb_ref, hng_ref = refs[:5]
        oh_ref, gab_ref, S_ref, st_sc = refs[5:]
    c_idx = pl.program_id(1)
    HD = H_EXPAND
    W = H_WIDTH

    @pl.when(c_idx == 0)
    def _():
        for h in range(H_HEADS):
            if has_state:
                st_sc[h] = S0_ref[h].T
            else:
                st_sc[h] = jnp.zeros((HD, HD), F32)

    if has_state:
        q_pre = qf_ref[:, 0:W].astype(F32)
        f = qf_ref[:, W:2 * W].astype(F32)
        v = ig_ref[:, 0:W].astype(F32)
        gz = ig_ref[:, W:2 * W].astype(F32)
    else:
        xh, _ = _rmsnorm_split(x_ref, ng_ref)
        q_pre = _dot(xh, wb_ref[:, 0:W])
        f = _dot(xh, wb_ref[:, W:2 * W])
        v = _dot(xh, wb_ref[:, 2 * W:3 * W])
        gz = _dot(xh, wb_ref[:, 3 * W:4 * W])
        gab_ref[...] = _dot(xh, wb_ref[:, 4 * W:N_B]).astype(BF16)
    v_bf = v.astype(BF16)

    lbp = lb_ref[...]
    lbm = jnp.max(lbp, axis=0, keepdims=True)
    lbe = jnp.exp(lbp - lbm)
    lb = lbe[0:1, :] / jnp.sum(lbe, axis=0, keepdims=True)

    e = jnp.exp(-jnp.abs(f))
    r = 1.0 / (1.0 + e)
    sig_pos = jnp.where(f >= 0, r, e * r)
    sig_neg = jnp.where(f >= 0, e * r, r)
    fgt = lb + (1.0 - lb) * sig_pos
    lf = jnp.log(fgt)
    kk = (1.0 - lb) * sig_neg
    qa = _silu(q_pre)

    tril = _tri(L, True)
    g = sum(_dot(tril, part) for part in _split3(lf))
    g_last = g[L - 1:L, :]
    q_in = (qa * jnp.exp(g)).astype(BF16)
    k_out = (kk * jnp.exp(g_last - g)).astype(BF16)
    s_dec = jnp.exp(g_last)

    row_i = lax.broadcasted_iota(jnp.int32, (L, L), 0)
    col_i = lax.broadcasted_iota(jnp.int32, (L, L), 1)

    levels = []
    hb = HGRN_BAND
    while hb < L:
        nb = L // (2 * hb)
        g3 = g.reshape(nb, 2 * hb, W)
        ge = jnp.broadcast_to(g3[:, hb - 1:hb, :], (nb, 2 * hb, W)).reshape(L, W)
        e_l = jnp.exp(-jnp.abs(g - ge))
        q_l = (qa * e_l).astype(BF16)
        k_l = (kk * e_l).astype(BF16)
        sh = (2 * hb).bit_length() - 1
        same_blk = jnp.right_shift(row_i, sh) == jnp.right_shift(col_i, sh)
        mask = same_blk & (jnp.bitwise_and(row_i, hb) != 0) & (jnp.bitwise_and(col_i, hb) == 0)
        levels.append((q_l, k_l, mask))
        hb *= 2

    nblk = L // HGRN_BAND
    to_blocks = lambda a: a.reshape(nblk, HGRN_BAND, W)
    sub_i = lax.broadcasted_iota(jnp.int32, (nblk, HGRN_BAND, W), 1)
    f0 = jnp.where(sub_i == 0, 0.0, to_blocks(fgt))
    qa3 = to_blocks(qa)
    ke = to_blocks(kk)
    vs = to_blocks(v)
    o_band = [None] * H_HEADS
    for d in range(HGRN_BAND):
        if d > 0:
            ke = f0 * pltpu.roll(ke, 1, axis=1)
            vs = pltpu.roll(vs, 1, axis=1)
        p_d = qa3 * ke
        for h in range(H_HEADS):
            hs = slice(h * HD, (h + 1) * HD)
            term = jnp.sum(p_d[:, :, hs], axis=2, keepdims=True) * vs[:, :, hs]
            o_band[h] = term if d == 0 else o_band[h] + term

    for h in range(H_HEADS):
        hs = slice(h * HD, (h + 1) * HD)
        st = st_sc[h]
        o = _dot_nt(q_in[:, hs], st.astype(BF16))
        a = jnp.zeros((L, L), F32)
        for q_l, k_l, mask in levels:
            a = a + jnp.where(mask, _dot_nt(q_l[:, hs], k_l[:, hs]), 0.0)
        o = o + _dot(a.astype(BF16), v_bf[:, hs]) + o_band[h].reshape(L, HD)
        st_sc[h] = st * s_dec[:, hs] + _dot_tn(v_bf[:, hs], k_out[:, hs])
        y = o * lax.rsqrt(jnp.mean(o * o, axis=1, keepdims=True) + EPS) * hng_ref[:, hs]
        oh_ref[:, hs] = (y * _silu(gz[:, hs])).astype(BF16)

    @pl.when(c_idx == pl.num_programs(1) - 1)
    def _():
        for h in range(H_HEADS):
            S_ref[h] = st_sc[h].T


def _hgrn(src, w, S0, B, T, L):
    NC = T // L
    has_state = S0 is not None
    tok = lambda b, c: (b * NC + c, 0)
    s_spec = pl.BlockSpec((None, H_HEADS, H_EXPAND, H_EXPAND), lambda b, c: (b, 0, 0, 0))
    out_specs = [pl.BlockSpec((L, H_WIDTH), tok)]
    out_shape = [jax.ShapeDtypeStruct((B * T, H_WIDTH), BF16)]
    if has_state:
        off = N_A // (2 * H_WIDTH)
        args = [src, src, w["hgrn_lb"], w["hnorm_g"], S0]
        in_specs = [
            pl.BlockSpec((L, 2 * H_WIDTH), lambda b, c: (b * NC + c, off)),
            pl.BlockSpec((L, 2 * H_WIDTH), lambda b, c: (b * NC + c, off + 1)),
            _resident(w["hgrn_lb"].shape), _resident(w["hnorm_g"].shape), s_spec,
        ]
    else:
        consts = [w["norm_g"], w["w_b"], w["hgrn_lb"], w["hnorm_g"]]
        args = [src] + consts
        in_specs = [pl.BlockSpec((L, D_MODEL), tok)] + [_resident(a.shape) for a in consts]
        out_specs.append(pl.BlockSpec((L, 2 * D_MODEL), tok))
        out_shape.append(jax.ShapeDtypeStruct((B * T, 2 * D_MODEL), BF16))
    return pl.pallas_call(
        functools.partial(_hgrn_kernel, L=L, has_state=has_state),
        grid=(B, NC),
        in_specs=in_specs,
        out_specs=out_specs + [s_spec],
        out_shape=out_shape + [jax.ShapeDtypeStruct((B, H_HEADS, H_EXPAND, H_EXPAND), F32)],
        scratch_shapes=[pltpu.VMEM((H_HEADS, H_EXPAND, H_EXPAND), F32)],
        compiler_params=pltpu.CompilerParams(
            dimension_semantics=("arbitrary", "arbitrary"), vmem_limit_bytes=VMEM_LIMIT),
        name="hgrn_state" if has_state else "hgrn",
    )(*args)


def _merge_kernel(x_ref, p_ref, hm_ref, oh_ref, gab_ref, wbrm_ref, wbrh_ref, wout_ref, wpg_ref, wple_ref,
                  fg_ref, o_ref):
    ya = _dot(hm_ref[...], wbrm_ref[...])
    yb = _dot(oh_ref[...], wbrh_ref[...])
    ga = gab_ref[:, 0:D_MODEL].astype(F32)
    gb = gab_ref[:, D_MODEL:2 * D_MODEL].astype(F32)
    y = _sigmoid(ga) * ya + _sigmoid(gb) * yb
    h1 = x_ref[...] + _dot(y.astype(BF16), wout_ref[...])
    pe = _dot(p_ref[...].astype(BF16), wple_ref[...])
    h2 = h1 + _sigmoid(_dot(h1.astype(BF16), wpg_ref[...])) * pe
    o_ref[...] = h2 * lax.rsqrt(jnp.mean(h2 * h2, axis=-1, keepdims=True) + EPS) * fg_ref[...]


def _merge(x2, p2, hm, oh, gab, gab_blk, w, tm):
    n_tok = x2.shape[0]
    tok = lambda i: (i, 0)
    consts = [w["w_brm"], w["w_brh"], w["w_out"], w["w_pg"], w["w_ple"], w["final_g"]]
    return pl.pallas_call(
        _merge_kernel,
        grid=(n_tok // tm,),
        in_specs=[
            pl.BlockSpec((tm, D_MODEL), tok),
            pl.BlockSpec((tm, PLE_DIM), tok),
            pl.BlockSpec((tm, M_INNER), tok),
            pl.BlockSpec((tm, H_WIDTH), tok),
            pl.BlockSpec((tm, 2 * D_MODEL), lambda i: (i, gab_blk)),
        ] + [_resident(a.shape) for a in consts],
        out_specs=pl.BlockSpec((tm, D_MODEL), tok),
        out_shape=jax.ShapeDtypeStruct((n_tok, D_MODEL), F32),
        compiler_params=pltpu.CompilerParams(
            dimension_semantics=("arbitrary",), vmem_limit_bytes=VMEM_LIMIT),
        name="merge",
    )(x2, p2, hm, oh, gab, *consts)


def _trunk(x, p, state, w, L, tm):
    B, T, _ = x.shape
    x2 = x.reshape(B * T, D_MODEL)
    p2 = p.reshape(B * T, PLE_DIM)
    if state is None:
        hm, conv_n, C_n, n_n, m_n = _mlstm(x2, w, None, B, T, L)
        oh, gab, S_n = _hgrn(x2, w, None, B, T, L)
        gab_blk = 0
    else:
        conv0, C0, n0, m0, S0 = state
        proj, gates = _proj(x2, w["norm_g"], w["w_a"], w["w_b"], w["wg_hi"], w["wg_lo"], tm, 2 * D_MODEL)
        m0p = jnp.zeros((B, SUBLANES, GATE_LANES), F32).at[:, :M_HEADS, :].set(m0[:, :, None])
        m_state = (conv0, C0, n0.reshape(B, M_HEADS, 1, M_HD), m0p)
        hm, conv_n, C_n, n_n, m_n = _mlstm((proj, gates), w, m_state, B, T, L)
        oh, S_n = _hgrn(proj, w, S0, B, T, L)
        gab, gab_blk = proj, (N_A + 4 * H_WIDTH) // (2 * D_MODEL)
    y = _merge(x2, p2, hm, oh, gab, gab_blk, w, tm)
    return (y.reshape(B, T, D_MODEL), conv_n[None], C_n[None], n_n.reshape(B, M_HEADS, M_HD)[None],
            m_n[:, :M_HEADS, 0][None], S_n[None])


def _prep_weights(norm_g, w_in, b_ig, b_fg, conv_w, conv_b, w_qm, w_km, mnorm_g, m_skip, w_brm, hgrn_lb, hnorm_g,
                  w_brh, w_out, w_ple, w_pg, final_g):
    assert norm_g.shape[0] == 1, "single-layer trunk"
    g1 = N_A + 2 * M_HEADS
    w_in0 = w_in[0]
    w_gate = w_in0[:, N_A:N_A + GATE_LANES]
    wg_hi = w_gate.astype(BF16)
    return {
        "norm_g": norm_g[0][None, :],
        "w_a": w_in0[:, :N_A].astype(BF16),
        "w_b": w_in0[:, g1:].astype(BF16),
        "wg_hi": wg_hi,
        "wg_lo": (w_gate - wg_hi.astype(F32)).astype(BF16),
        "gbias": jnp.pad(jnp.concatenate([b_ig[0], b_fg[0]]), (0, GATE_LANES - 2 * M_HEADS))[None, :],
        "conv_w": conv_w[0],
        "conv_b": conv_b[0][None, :],
        "w_qm": w_qm[0].astype(BF16),
        "w_km": w_km[0].astype(BF16),
        "mnorm_g": mnorm_g[0][None, :],
        "m_skip": m_skip[0][None, :],
        "w_brm": w_brm[0].astype(BF16),
        "hgrn_lb": hgrn_lb,
        "hnorm_g": hnorm_g[0][None, :],
        "w_brh": w_brh[0].astype(BF16),
        "w_out": w_out[0].astype(BF16),
        "w_ple": w_ple[0].astype(BF16),
        "w_pg": w_pg[0].astype(BF16),
        "final_g": final_g[None, :],
    }


def kernel(x_prompt, x_sample, state_conv, state_mlstm_C, state_mlstm_n, state_mlstm_m, state_hgrn, p_prompt, p_sample, norm_g, w_in, b_ig, b_fg, conv_w, conv_b, w_qm, w_km, mnorm_g, m_skip, w_brm, hgrn_lb, hnorm_g, w_brh, w_out, w_ple, w_pg, final_g):
    w = _prep_weights(norm_g, w_in, b_ig, b_fg, conv_w, conv_b, w_qm, w_km, mnorm_g, m_skip, w_brm, hgrn_lb,
                      hnorm_g, w_brh, w_out, w_ple, w_pg, final_g)
    out_p = _trunk(x_prompt, p_prompt[0], None, w, L=256, tm=512)
    state = (state_conv[0], state_mlstm_C[0], state_mlstm_n[0], state_mlstm_m[0], state_hgrn[0])
    out_s = _trunk(x_sample, p_sample[0], state, w, L=x_sample.shape[1], tm=512)
    return (out_p[0], out_s[0]) + tuple(out_p[1:]) + tuple(out_s[1:])
```

```python
import functools

import jax
import jax.numpy as jnp
from jax import lax
from jax.experimental import pallas as pl
from jax.experimental.pallas import tpu as pltpu

F32 = jnp.float32
BF16 = jnp.bfloat16

D_MODEL = 1024
PLE_DIM = 256
M_HEADS = 4
M_INNER = 2 * D_MODEL
M_HD = M_INNER // M_HEADS
CONV_W = 4
H_EXPAND = 128
H_WIDTH = D_MODEL
H_HEADS = H_WIDTH // H_EXPAND
EPS = 1e-6

N_A = 3 * M_INNER
N_B = 4 * H_WIDTH + 2 * D_MODEL
GATE_LANES = 128
SUBLANES = 8
HGRN_BAND = SUBLANES
VMEM_LIMIT = 56 * 1024 * 1024


def _dot(a, b):
    return jnp.dot(a, b, preferred_element_type=F32)


def _dot_nt(a, b):
    return lax.dot_general(a, b, (((1,), (1,)), ((), ())), preferred_element_type=F32)


def _dot_tn(a, b):
    return lax.dot_general(a, b, (((0,), (0,)), ((), ())), preferred_element_type=F32)


def _split3(x):
    hi = x.astype(BF16)
    r = x - hi.astype(F32)
    mid = r.astype(BF16)
    lo = (r - mid.astype(F32)).astype(BF16)
    return hi, mid, lo


def _sigmoid(x):
    return 1.0 / (1.0 + jnp.exp(-x))


def _silu(x):
    return x * _sigmoid(x)


def _log_sigmoid(x):
    return jnp.minimum(x, 0.0) - jnp.log(1.0 + jnp.exp(-jnp.abs(x)))


def _tri(n, lower):
    r = lax.broadcasted_iota(jnp.int32, (n, n), 0)
    c = lax.broadcasted_iota(jnp.int32, (n, n), 1)
    return jnp.where((r >= c) if lower else (r <= c), 1.0, 0.0).astype(BF16)


def _rmsnorm_split(x_ref, g_ref):
    x = x_ref[...]
    xn = x * lax.rsqrt(jnp.mean(x * x, axis=-1, keepdims=True) + EPS) * g_ref[...]
    hi = xn.astype(BF16)
    lo = (xn - hi.astype(F32)).astype(BF16)
    return hi, lo


def _gate_dot(hi, lo, wgh_ref, wgl_ref):
    return _dot_nt(hi, wgh_ref[...]) + _dot_nt(lo, wgh_ref[...]) + _dot_nt(hi, wgl_ref[...])


def _resident(shape):
    return pl.BlockSpec(shape, lambda *_: (0,) * len(shape), pipeline_mode=pl.Buffered(1))


def _proj_kernel(x_ref, g_ref, wa_ref, wb_ref, wgh_ref, wgl_ref, o_ref, gate_ref, xh_sc, *, n_a):
    j = pl.program_id(1)

    @pl.when(j == 0)
    def _():
        hi, lo = _rmsnorm_split(x_ref, g_ref)
        xh_sc[...] = hi
        gate_ref[...] = _gate_dot(hi, lo, wgh_ref, wgl_ref)

    @pl.when(j < n_a)
    def _():
        o_ref[...] = _dot_nt(xh_sc[...], wa_ref[...]).astype(BF16)

    @pl.when(j >= n_a)
    def _():
        o_ref[...] = _dot_nt(xh_sc[...], wb_ref[...]).astype(BF16)


def _proj(x2, w, tm, tn):
    n_tok = x2.shape[0]
    n_a = N_A // tn
    return pl.pallas_call(
        functools.partial(_proj_kernel, n_a=n_a),
        grid=(n_tok // tm, (N_A + N_B) // tn),
        in_specs=[
            pl.BlockSpec((tm, D_MODEL), lambda i, j: (i, 0)),
            pl.BlockSpec((1, D_MODEL), lambda i, j: (0, 0)),
            pl.BlockSpec((tn, D_MODEL), lambda i, j: (jnp.minimum(j, n_a - 1), 0)),
            pl.BlockSpec((tn, D_MODEL), lambda i, j: (jnp.maximum(j - n_a, 0), 0)),
            pl.BlockSpec((GATE_LANES, D_MODEL), lambda i, j: (0, 0)),
            pl.BlockSpec((GATE_LANES, D_MODEL), lambda i, j: (0, 0)),
        ],
        out_specs=[
            pl.BlockSpec((tm, tn), lambda i, j: (i, j)),
            pl.BlockSpec((tm, GATE_LANES), lambda i, j: (i, 0)),
        ],
        out_shape=[
            jax.ShapeDtypeStruct((n_tok, N_A + N_B), BF16),
            jax.ShapeDtypeStruct((n_tok, GATE_LANES), F32),
        ],
        scratch_shapes=[pltpu.VMEM((tm, D_MODEL), BF16)],
        compiler_params=pltpu.CompilerParams(
            dimension_semantics=("arbitrary", "arbitrary"), vmem_limit_bytes=VMEM_LIMIT),
        name="proj",
    )(x2, w["norm_g"], w["wt_a"], w["wt_b"], w["wgt_hi"], w["wgt_lo"])


def _mlstm_chunk(uext_sc, gates, v_of, z_of, Cin_ref, nin_ref, min_ref, convw_ref, convb_ref, wq_ref, wk_ref,
                 mng_ref, skip_ref, hm_ref, convo_ref, C_ref, n_ref, m_ref, L):
    TAIL = CONV_W - 1
    BASE = SUBLANES

    ext = uext_sc[...]
    conv = convb_ref[...] + ext[BASE:, :] * convw_ref[TAIL:CONV_W, :]
    for j in range(TAIL):
        conv = conv + pltpu.roll(ext, TAIL - j, axis=0)[BASE:, :] * convw_ref[j:j + 1, :]
    tail = ext[BASE + L - TAIL:, :]
    uext_sc[BASE - TAIL:BASE, :] = tail
    convo_ref[...] = tail
    c_act = _silu(conv)

    lf_cols = _log_sigmoid(gates)
    gates_t = gates.T[0:2 * M_HEADS, :]
    lf_rows = _log_sigmoid(gates_t)
    tril, triu = _tri(L, True), _tri(L, False)
    b_cols = sum(_dot(tril, part) for part in _split3(lf_cols))
    b_rows = sum(_dot(part, triu) for part in _split3(lf_rows))
    row_i = lax.broadcasted_iota(jnp.int32, (L, L), 0)
    col_i = lax.broadcasted_iota(jnp.int32, (L, L), 1)
    causal = row_i >= col_i

    for h in range(M_HEADS):
        hs = slice(h * M_HD, (h + 1) * M_HD)
        ch = c_act[:, hs].astype(BF16)
        q = _dot(ch, wq_ref[h])
        k = _dot(ch, wk_ref[h]) * (M_HD ** -0.5)
        v = v_of(h)
        q_bf = q.astype(BF16)
        k_bf = k.astype(BF16)

        ig_col = gates[:, h:h + 1]
        ig_row = gates_t[h:h + 1, :]
        b_col = b_cols[:, M_HEADS + h:M_HEADS + h + 1]
        b_row = b_rows[M_HEADS + h:M_HEADS + h + 1, :]
        m_c = min_ref[h:h + 1, 0:1]

        logd = jnp.where(causal, b_col - b_row + ig_row, -jnp.inf)
        m_prev = b_col + m_c
        m_t = jnp.maximum(m_prev, jnp.max(logd, axis=1, keepdims=True))
        dmat = jnp.exp(logd - m_t)
        w_inter = jnp.exp(m_prev - m_t)

        C = Cin_ref[h]
        n = nin_ref[h]
        s_mat = _dot_nt(q_bf, k_bf) * dmat
        num = w_inter * _dot_nt(q_bf, C.astype(BF16)) + _dot(s_mat.astype(BF16), v.astype(BF16))
        den = w_inter * jnp.sum(q * n, axis=1, keepdims=True) + jnp.sum(s_mat, axis=1, keepdims=True)
        hh = num * (1.0 / jnp.maximum(jnp.abs(den), jnp.exp(-m_t)))

        m_last = m_t[L - 1:L, :]
        w_last = jnp.exp(b_col[L - 1:L, :] - b_col + ig_col - m_last)
        decay = jnp.exp(m_prev[L - 1:L, :] - m_last)
        wv = (w_last * v).astype(BF16)
        C_ref[h] = decay * C + _dot_tn(wv, k_bf)
        n_ref[h] = decay * n + jnp.sum(w_last * k, axis=0, keepdims=True)
        m_ref[h:h + 1, :] = jnp.broadcast_to(m_last, (1, m_ref.shape[1]))

        mu = jnp.mean(hh, axis=1, keepdims=True)
        d = hh - mu
        y = d * lax.rsqrt(jnp.mean(d * d, axis=1, keepdims=True) + EPS) * mng_ref[:, hs]
        hm_ref[:, hs] = ((y + skip_ref[:, hs] * c_act[:, hs]) * _silu(z_of(h))).astype(BF16)


def _mlstm_state_kernel(pm_ref, gate_ref, gbias_ref, convw_ref, convb_ref, wq_ref, wk_ref, mng_ref, skip_ref,
                        conv0_ref, C0_ref, n0_ref, m0_ref, hm_ref, convo_ref, C_ref, n_ref, m_ref, uext_sc, *, L):
    uext_sc[SUBLANES - (CONV_W - 1):SUBLANES, :] = conv0_ref[...]
    uext_sc[SUBLANES:SUBLANES + L, :] = pm_ref[:, 0:M_INNER].astype(F32)
    v_of = lambda h: pm_ref[:, M_INNER + h * M_HD:M_INNER + (h + 1) * M_HD].astype(F32)
    z_of = lambda h: pm_ref[:, 2 * M_INNER + h * M_HD:2 * M_INNER + (h + 1) * M_HD].astype(F32)
    _mlstm_chunk(uext_sc, gate_ref[...] + gbias_ref[...], v_of, z_of, C0_ref, n0_ref, m0_ref, convw_ref, convb_ref,
                 wq_ref, wk_ref, mng_ref, skip_ref, hm_ref, convo_ref, C_ref, n_ref, m_ref, L)
    m_ref[M_HEADS:, :] = m0_ref[M_HEADS:, :]


def _mlstm_kernel(x_ref, ng_ref, wa_ref, wgh_ref, wgl_ref, gbias_ref, convw_ref, convb_ref,
                  wq_ref, wk_ref, mng_ref, skip_ref, hm_ref, convo_ref, C_ref, n_ref, m_ref, uext_sc, *, L):
    @pl.when(pl.program_id(1) == 0)
    def _():
        uext_sc[SUBLANES - (CONV_W - 1):SUBLANES, :] = jnp.zeros((CONV_W - 1, M_INNER), F32)
        C_ref[...] = jnp.zeros(C_ref.shape, F32)
        n_ref[...] = jnp.zeros(n_ref.shape, F32)
        m_ref[...] = jnp.zeros(m_ref.shape, F32)

    xh, xl = _rmsnorm_split(x_ref, ng_ref)
    uext_sc[SUBLANES:SUBLANES + L, :] = _dot_nt(xh, wa_ref[0:M_INNER, :])
    gates = _gate_dot(xh, xl, wgh_ref, wgl_ref) + gbias_ref[...]
    v_of = lambda h: _dot_nt(xh, wa_ref[M_INNER + h * M_HD:M_INNER + (h + 1) * M_HD, :])
    z_of = lambda h: _dot_nt(xh, wa_ref[2 * M_INNER + h * M_HD:2 * M_INNER + (h + 1) * M_HD, :])
    _mlstm_chunk(uext_sc, gates, v_of, z_of, C_ref, n_ref, m_ref, convw_ref, convb_ref,
                 wq_ref, wk_ref, mng_ref, skip_ref, hm_ref, convo_ref, C_ref, n_ref, m_ref, L)


def _state_specs(index):
    return [
        pl.BlockSpec((None, CONV_W - 1, M_INNER), lambda *g: (index(*g), 0, 0)),
        pl.BlockSpec((None, M_HEADS, M_HD, M_HD), lambda *g: (index(*g), 0, 0, 0)),
        pl.BlockSpec((None, M_HEADS, 1, M_HD), lambda *g: (index(*g), 0, 0, 0)),
        pl.BlockSpec((None, SUBLANES, GATE_LANES), lambda *g: (index(*g), 0, 0)),
    ]


def _mlstm_out_shape(B, T):
    return [
        jax.ShapeDtypeStruct((B * T, M_INNER), BF16),
        jax.ShapeDtypeStruct((B, CONV_W - 1, M_INNER), F32),
        jax.ShapeDtypeStruct((B, M_HEADS, M_HD, M_HD), F32),
        jax.ShapeDtypeStruct((B, M_HEADS, 1, M_HD), F32),
        jax.ShapeDtypeStruct((B, SUBLANES, GATE_LANES), F32),
    ]


def _mlstm_shared(w):
    return [w["gbias"], w["conv_w"], w["conv_b"], w["w_qm"], w["w_km"], w["mnorm_g"], w["m_skip"]]


def _mlstm_state(proj, gates, w, state, B, L):
    shared = _mlstm_shared(w)
    tok = lambda b: (b, 0)
    state_specs = _state_specs(lambda b: b)
    return pl.pallas_call(
        functools.partial(_mlstm_state_kernel, L=L),
        grid=(B,),
        in_specs=[pl.BlockSpec((L, N_A), tok), pl.BlockSpec((L, GATE_LANES), tok)]
        + [_resident(a.shape) for a in shared] + state_specs,
        out_specs=[pl.BlockSpec((L, M_INNER), tok)] + state_specs,
        out_shape=_mlstm_out_shape(B, L),
        scratch_shapes=[pltpu.VMEM((SUBLANES + L, M_INNER), F32)],
        compiler_params=pltpu.CompilerParams(dimension_semantics=("arbitrary",), vmem_limit_bytes=VMEM_LIMIT),
        name="mlstm_state",
    )(proj, gates, *shared, *state)


def _mlstm(x2, w, B, T, L):
    NC = T // L
    tok = lambda b, c: (b * NC + c, 0)
    consts = [w["norm_g"], w["wt_a"], w["wgt_hi"], w["wgt_lo"]] + _mlstm_shared(w)
    return pl.pallas_call(
        functools.partial(_mlstm_kernel, L=L),
        grid=(B, NC),
        in_specs=[pl.BlockSpec((L, D_MODEL), tok)] + [_resident(a.shape) for a in consts],
        out_specs=[pl.BlockSpec((L, M_INNER), tok)] + _state_specs(lambda b, c: b),
        out_shape=_mlstm_out_shape(B, T),
        scratch_shapes=[pltpu.VMEM((SUBLANES + L, M_INNER), F32)],
        compiler_params=pltpu.CompilerParams(
            dimension_semantics=("arbitrary", "arbitrary"), vmem_limit_bytes=VMEM_LIMIT),
        name="mlstm",
    )(x2, *consts)


def _hgrn_chunk(q_pre, f, v, gz, lb_ref, hng_ref, st_sc, oh_ref, L):
    HD = H_EXPAND
    W = H_WIDTH
    v_bf = v.astype(BF16)

    lbp = lb_ref[...]
    lbm = jnp.max(lbp, axis=0, keepdims=True)
    lbe = jnp.exp(lbp - lbm)
    lb = lbe[0:1, :] / jnp.sum(lbe, axis=0, keepdims=True)

    e = jnp.exp(-jnp.abs(f))
    r = 1.0 / (1.0 + e)
    sig_pos = jnp.where(f >= 0, r, e * r)
    sig_neg = jnp.where(f >= 0, e * r, r)
    fgt = lb + (1.0 - lb) * sig_pos
    lf = jnp.log(fgt)
    kk = (1.0 - lb) * sig_neg
    qa = _silu(q_pre)

    tril = _tri(L, True)
    g = sum(_dot(tril, part) for part in _split3(lf))
    g_last = g[L - 1:L, :]
    q_in = (qa * jnp.exp(g)).astype(BF16)
    k_out = (kk * jnp.exp(g_last - g)).astype(BF16)
    s_dec = jnp.exp(g_last)

    row_i = lax.broadcasted_iota(jnp.int32, (L, L), 0)
    col_i = lax.broadcasted_iota(jnp.int32, (L, L), 1)

    levels = []
    hb = HGRN_BAND
    while hb < L:
        nb = L // (2 * hb)
        g3 = g.reshape(nb, 2 * hb, W)
        ge = jnp.broadcast_to(g3[:, hb - 1:hb, :], (nb, 2 * hb, W)).reshape(L, W)
        e_l = jnp.exp(-jnp.abs(g - ge))
        q_l = (qa * e_l).astype(BF16)
        k_l = (kk * e_l).astype(BF16)
        sh = (2 * hb).bit_length() - 1
        same_blk = jnp.right_shift(row_i, sh) == jnp.right_shift(col_i, sh)
        mask = same_blk & (jnp.bitwise_and(row_i, hb) != 0) & (jnp.bitwise_and(col_i, hb) == 0)
        levels.append((q_l, k_l, mask))
        hb *= 2

    nblk = L // HGRN_BAND
    to_blocks = lambda a: a.reshape(nblk, HGRN_BAND, W)
    sub_i = lax.broadcasted_iota(jnp.int32, (nblk, HGRN_BAND, W), 1)
    f0 = jnp.where(sub_i == 0, 0.0, to_blocks(fgt))
    qa3 = to_blocks(qa)
    ke = to_blocks(kk)
    vs = to_blocks(v)
    o_band = [None] * H_HEADS
    for d in range(HGRN_BAND):
        if d > 0:
            ke = f0 * pltpu.roll(ke, 1, axis=1)
            vs = pltpu.roll(vs, 1, axis=1)
        p_d = qa3 * ke
        for h in range(H_HEADS):
            hs = slice(h * HD, (h + 1) * HD)
            term = jnp.sum(p_d[:, :, hs], axis=2, keepdims=True) * vs[:, :, hs]
            o_band[h] = term if d == 0 else o_band[h] + term

    for h in range(H_HEADS):
        hs = slice(h * HD, (h + 1) * HD)
        st = st_sc[h]
        o = _dot_nt(q_in[:, hs], st.astype(BF16))
        a = jnp.zeros((L, L), F32)
        for q_l, k_l, mask in levels:
            a = a + jnp.where(mask, _dot_nt(q_l[:, hs], k_l[:, hs]), 0.0)
        o = o + _dot(a.astype(BF16), v_bf[:, hs]) + o_band[h].reshape(L, HD)
        st_sc[h] = st * s_dec[:, hs] + _dot_tn(v_bf[:, hs], k_out[:, hs])
        y = o * lax.rsqrt(jnp.mean(o * o, axis=1, keepdims=True) + EPS) * hng_ref[:, hs]
        oh_ref[:, hs] = (y * _silu(gz[:, hs])).astype(BF16)


def _merge_math(x, p, hm, oh, gab, wbrm_ref, wbrh_ref, wout_ref, wpg_ref, wple_ref, fg_ref):
    ya = _dot(hm, wbrm_ref[...])
    yb = _dot(oh, wbrh_ref[...])
    ga = gab[:, 0:D_MODEL].astype(F32)
    gb = gab[:, D_MODEL:2 * D_MODEL].astype(F32)
    y = _sigmoid(ga) * ya + _sigmoid(gb) * yb
    h1 = x + _dot(y.astype(BF16), wout_ref[...])
    pe = _dot(p.astype(BF16), wple_ref[...])
    h2 = h1 + _sigmoid(_dot(h1.astype(BF16), wpg_ref[...])) * pe
    return h2 * lax.rsqrt(jnp.mean(h2 * h2, axis=-1, keepdims=True) + EPS) * fg_ref[...]


def _hgrn_state_kernel(qf_ref, ig_ref, lb_ref, hng_ref, S0_ref, oh_ref, S_ref, st_sc, *, L):
    W = H_WIDTH
    for h in range(H_HEADS):
        st_sc[h] = S0_ref[h].T
    _hgrn_chunk(qf_ref[:, 0:W].astype(F32), qf_ref[:, W:2 * W].astype(F32), ig_ref[:, 0:W].astype(F32),
                ig_ref[:, W:2 * W].astype(F32), lb_ref, hng_ref, st_sc, oh_ref, L)
    for h in range(H_HEADS):
        S_ref[h] = st_sc[h].T


def _hgrn_state(proj, w, S0, B, L):
    off = N_A // (2 * H_WIDTH)
    s_spec = pl.BlockSpec((None, H_HEADS, H_EXPAND, H_EXPAND), lambda b: (b, 0, 0, 0))
    return pl.pallas_call(
        functools.partial(_hgrn_state_kernel, L=L),
        grid=(B,),
        in_specs=[
            pl.BlockSpec((L, 2 * H_WIDTH), lambda b: (b, off)),
            pl.BlockSpec((L, 2 * H_WIDTH), lambda b: (b, off + 1)),
            _resident(w["hgrn_lb"].shape), _resident(w["hnorm_g"].shape), s_spec,
        ],
        out_specs=[pl.BlockSpec((L, H_WIDTH), lambda b: (b, 0)), s_spec],
        out_shape=[jax.ShapeDtypeStruct((B * L, H_WIDTH), BF16),
                   jax.ShapeDtypeStruct((B, H_HEADS, H_EXPAND, H_EXPAND), F32)],
        scratch_shapes=[pltpu.VMEM((H_HEADS, H_EXPAND, H_EXPAND), F32)],
        compiler_params=pltpu.CompilerParams(dimension_semantics=("arbitrary",), vmem_limit_bytes=VMEM_LIMIT),
        name="hgrn_state",
    )(proj, proj, w["hgrn_lb"], w["hnorm_g"], S0)


def _hgrn_kernel(x_ref, ng_ref, wb_ref, lb_ref, hng_ref, oh_ref, gab_ref, S_ref, st_sc, *, L):
    W = H_WIDTH

    @pl.when(pl.program_id(1) == 0)
    def _():
        st_sc[...] = jnp.zeros(st_sc.shape, F32)

    xh, _ = _rmsnorm_split(x_ref, ng_ref)
    q_pre = _dot_nt(xh, wb_ref[0:W, :])
    f = _dot_nt(xh, wb_ref[W:2 * W, :])
    v = _dot_nt(xh, wb_ref[2 * W:3 * W, :])
    gz = _dot_nt(xh, wb_ref[3 * W:4 * W, :])
    gab_ref[...] = _dot_nt(xh, wb_ref[4 * W:N_B, :]).astype(BF16)
    _hgrn_chunk(q_pre, f, v, gz, lb_ref, hng_ref, st_sc, oh_ref, L)

    @pl.when(pl.program_id(1) == pl.num_programs(1) - 1)
    def _():
        for h in range(H_HEADS):
            S_ref[h] = st_sc[h].T


def _hgrn(x2, w, B, T, L):
    NC = T // L
    tok = lambda b, c: (b * NC + c, 0)
    consts = [w["norm_g"], w["wt_b"], w["hgrn_lb"], w["hnorm_g"]]
    return pl.pallas_call(
        functools.partial(_hgrn_kernel, L=L),
        grid=(B, NC),
        in_specs=[pl.BlockSpec((L, D_MODEL), tok)] + [_resident(a.shape) for a in consts],
        out_specs=[pl.BlockSpec((L, H_WIDTH), tok), pl.BlockSpec((L, 2 * D_MODEL), tok),
                   pl.BlockSpec((None, H_HEADS, H_EXPAND, H_EXPAND), lambda b, c: (b, 0, 0, 0))],
        out_shape=[jax.ShapeDtypeStruct((B * T, H_WIDTH), BF16),
                   jax.ShapeDtypeStruct((B * T, 2 * D_MODEL), BF16),
                   jax.ShapeDtypeStruct((B, H_HEADS, H_EXPAND, H_EXPAND), F32)],
        scratch_shapes=[pltpu.VMEM((H_HEADS, H_EXPAND, H_EXPAND), F32)],
        compiler_params=pltpu.CompilerParams(
            dimension_semantics=("arbitrary", "arbitrary"), vmem_limit_bytes=VMEM_LIMIT),
        name="hgrn",
    )(x2, *consts)


def _merge_kernel(x_ref, p_ref, hm_ref, oh_ref, gab_ref, wbrm_ref, wbrh_ref, wout_ref, wpg_ref, wple_ref,
                  fg_ref, o_ref):
    o_ref[...] = _merge_math(x_ref[...], p_ref[...], hm_ref[...], oh_ref[...], gab_ref[...],
                             wbrm_ref, wbrh_ref, wout_ref, wpg_ref, wple_ref, fg_ref)


def _merge(x2, p2, hm, oh, gab, gab_blk, w, tm):
    n_tok = x2.shape[0]
    tok = lambda i: (i, 0)
    consts = [w["w_brm"], w["w_brh"], w["w_out"], w["w_pg"], w["w_ple"], w["final_g"]]
    return pl.pallas_call(
        _merge_kernel,
        grid=(n_tok // tm,),
        in_specs=[
            pl.BlockSpec((tm, D_MODEL), tok),
            pl.BlockSpec((tm, PLE_DIM), tok),
            pl.BlockSpec((tm, M_INNER), tok),
            pl.BlockSpec((tm, H_WIDTH), tok),
            pl.BlockSpec((tm, 2 * D_MODEL), lambda i: (i, gab_blk)),
        ] + [_resident(a.shape) for a in consts],
        out_specs=pl.BlockSpec((tm, D_MODEL), tok),
        out_shape=jax.ShapeDtypeStruct((n_tok, D_MODEL), F32),
        compiler_params=pltpu.CompilerParams(
            dimension_semantics=("arbitrary",), vmem_limit_bytes=VMEM_LIMIT),
        name="merge",
    )(x2, p2, hm, oh, gab, *consts)


def _trunk(x, p, state, w, L, tm):
    B, T, _ = x.shape
    x2 = x.reshape(B * T, D_MODEL)
    p2 = p.reshape(B * T, PLE_DIM)
    if state is None:
        hm, conv_n, C_n, n_n, m_n = _mlstm(x2, w, B, T, L)
        oh, gab, S_n = _hgrn(x2, w, B, T, L)
        y = _merge(x2, p2, hm, oh, gab, 0, w, tm)
    else:
        conv0, C0, n0, m0, S0 = state
        proj, gates = _proj(x2, w, tm, 2 * D_MODEL)
        m0p = jnp.zeros((B, SUBLANES, GATE_LANES), F32).at[:, :M_HEADS, :].set(m0[:, :, None])
        m_state = (conv0, C0, n0.reshape(B, M_HEADS, 1, M_HD), m0p)
        hm, conv_n, C_n, n_n, m_n = _mlstm_state(proj, gates, w, m_state, B, L)
        oh, S_n = _hgrn_state(proj, w, S0, B, L)
        y = _merge(x2, p2, hm, oh, proj, (N_A + 4 * H_WIDTH) // (2 * D_MODEL), w, tm)
    return (y.reshape(B, T, D_MODEL), conv_n[None], C_n[None], n_n.reshape(B, M_HEADS, M_HD)[None],
            m_n[:, :M_HEADS, 0][None], S_n[None])


def _prep_weights(norm_g, w_in, b_ig, b_fg, conv_w, conv_b, w_qm, w_km, mnorm_g, m_skip, w_brm, hgrn_lb, hnorm_g,
                  w_brh, w_out, w_ple, w_pg, final_g):
    assert norm_g.shape[0] == 1, "single-layer trunk"
    g1 = N_A + 2 * M_HEADS
    wt = jnp.swapaxes(w_in[0], 0, 1)
    wt_gate = wt[N_A:N_A + GATE_LANES]
    wgt_hi = wt_gate.astype(BF16)
    return {
        "norm_g": norm_g[0][None, :],
        "wt_a": wt[:N_A].astype(BF16),
        "wt_b": wt[g1:].astype(BF16),
        "wgt_hi": wgt_hi,
        "wgt_lo": (wt_gate - wgt_hi.astype(F32)).astype(BF16),
        "gbias": jnp.pad(jnp.concatenate([b_ig[0], b_fg[0]]), (0, GATE_LANES - 2 * M_HEADS))[None, :],
        "conv_w": conv_w[0],
        "conv_b": conv_b[0][None, :],
        "w_qm": w_qm[0].astype(BF16),
        "w_km": w_km[0].astype(BF16),
        "mnorm_g": mnorm_g[0][None, :],
        "m_skip": m_skip[0][None, :],
        "w_brm": w_brm[0].astype(BF16),
        "hgrn_lb": hgrn_lb,
        "hnorm_g": hnorm_g[0][None, :],
        "w_brh": w_brh[0].astype(BF16),
        "w_out": w_out[0].astype(BF16),
        "w_ple": w_ple[0].astype(BF16),
        "w_pg": w_pg[0].astype(BF16),
        "final_g": final_g[None, :],
    }


def kernel(x_prompt, x_sample, state_conv, state_mlstm_C, state_mlstm_n, state_mlstm_m, state_hgrn, p_prompt, p_sample, norm_g, w_in, b_ig, b_fg, conv_w, conv_b, w_qm, w_km, mnorm_g, m_skip, w_brm, hgrn_lb, hnorm_g, w_brh, w_out, w_ple, w_pg, final_g):
    w = _prep_weights(norm_g, w_in, b_ig, b_fg, conv_w, conv_b, w_qm, w_km, mnorm_g, m_skip, w_brm, hgrn_lb,
                      hnorm_g, w_brh, w_out, w_ple, w_pg, final_g)
    out_p = _trunk(x_prompt, p_prompt[0], None, w, L=256, tm=512)
    state = (state_conv[0], state_mlstm_C[0], state_mlstm_n[0], state_mlstm_m[0], state_hgrn[0])
    out_s = _trunk(x_sample, p_sample[0], state, w, L=x_sample.shape[1], tm=512)
    return (out_p[0], out_s[0]) + tuple(out_p[1:]) + tuple(out_s[1:])
```

```python
import functools
import math

import jax
import jax.numpy as jnp
from jax import lax
from jax.experimental import pallas as pl
from jax.experimental.pallas import tpu as pltpu

F32 = jnp.float32
BF16 = jnp.bfloat16

D_MODEL = 1024
PLE_DIM = 256
M_HEADS = 4
M_INNER = 2 * D_MODEL
M_HD = M_INNER // M_HEADS
CONV_W = 4
H_EXPAND = 128
H_WIDTH = D_MODEL
H_HEADS = H_WIDTH // H_EXPAND
EPS = 1e-6

N_A = 3 * M_INNER
N_B = 4 * H_WIDTH + 2 * D_MODEL
GATE_LANES = 128
SUBLANES = 8
HGRN_BAND = 4
VMEM_LIMIT = 56 * 1024 * 1024
PROMPT_CHUNK = 256
PROJ_TILE = 1024
MERGE_TILE = 512
MLSTM_STREAMS = 2
HGRN_STREAMS = 4


def _dot(a, b):
    return jnp.dot(a, b, preferred_element_type=F32)


def _dot_nt(a, b):
    return lax.dot_general(a, b, (((1,), (1,)), ((), ())), preferred_element_type=F32)


def _dot_tn(a, b):
    return lax.dot_general(a, b, (((0,), (0,)), ((), ())), preferred_element_type=F32)


def _split3(x):
    hi = x.astype(BF16)
    r = x - hi.astype(F32)
    mid = r.astype(BF16)
    lo = (r - mid.astype(F32)).astype(BF16)
    return hi, mid, lo


def _sigmoid(x):
    return 1.0 / (1.0 + jnp.exp(-x))


def _silu(x):
    return x * _sigmoid(x)


def _log_sigmoid(x):
    return jnp.minimum(x, 0.0) - jnp.log(1.0 + jnp.exp(-jnp.abs(x)))


def _tri(n, lower):
    r = lax.broadcasted_iota(jnp.int32, (n, n), 0)
    c = lax.broadcasted_iota(jnp.int32, (n, n), 1)
    return jnp.where((r >= c) if lower else (r <= c), 1.0, 0.0).astype(BF16)


def _rmsnorm_split(x_ref, g_ref):
    x = x_ref[...]
    xn = x * lax.rsqrt(jnp.mean(x * x, axis=-1, keepdims=True) + EPS) * g_ref[...]
    hi = xn.astype(BF16)
    lo = (xn - hi.astype(F32)).astype(BF16)
    return hi, lo


def _gate_dot(hi, lo, wgh_ref, wgl_ref):
    return _dot_nt(hi, wgh_ref[...]) + _dot_nt(lo, wgh_ref[...]) + _dot_nt(hi, wgl_ref[...])


def _resident(shape):
    return pl.BlockSpec(shape, lambda *_: (0,) * len(shape), pipeline_mode=pl.Buffered(1))


def _proj_kernel(x_ref, g_ref, wa_ref, wb_ref, wgh_ref, wgl_ref, o_ref, gate_ref, xh_sc, *, n_a):
    j = pl.program_id(1)

    @pl.when(j == 0)
    def _():
        hi, lo = _rmsnorm_split(x_ref, g_ref)
        xh_sc[...] = hi
        gate_ref[...] = _gate_dot(hi, lo, wgh_ref, wgl_ref)

    @pl.when(j < n_a)
    def _():
        o_ref[...] = _dot_nt(xh_sc[...], wa_ref[...]).astype(BF16)

    @pl.when(j >= n_a)
    def _():
        o_ref[...] = _dot_nt(xh_sc[...], wb_ref[...]).astype(BF16)


def _proj(x2, w, tm, tn):
    n_tok = x2.shape[0]
    n_a = N_A // tn
    return pl.pallas_call(
        functools.partial(_proj_kernel, n_a=n_a),
        grid=(n_tok // tm, (N_A + N_B) // tn),
        in_specs=[
            pl.BlockSpec((tm, D_MODEL), lambda i, j: (i, 0)),
            pl.BlockSpec((1, D_MODEL), lambda i, j: (0, 0)),
            pl.BlockSpec((tn, D_MODEL), lambda i, j: (jnp.minimum(j, n_a - 1), 0)),
            pl.BlockSpec((tn, D_MODEL), lambda i, j: (jnp.maximum(j - n_a, 0), 0)),
            pl.BlockSpec((GATE_LANES, D_MODEL), lambda i, j: (0, 0)),
            pl.BlockSpec((GATE_LANES, D_MODEL), lambda i, j: (0, 0)),
        ],
        out_specs=[
            pl.BlockSpec((tm, tn), lambda i, j: (i, j)),
            pl.BlockSpec((tm, GATE_LANES), lambda i, j: (i, 0)),
        ],
        out_shape=[
            jax.ShapeDtypeStruct((n_tok, N_A + N_B), BF16),
            jax.ShapeDtypeStruct((n_tok, GATE_LANES), F32),
        ],
        scratch_shapes=[pltpu.VMEM((tm, D_MODEL), BF16)],
        compiler_params=pltpu.CompilerParams(
            dimension_semantics=("arbitrary", "arbitrary"), vmem_limit_bytes=VMEM_LIMIT),
        name="proj",
    )(x2, w["norm_g"], w["wt_a"], w["wt_b"], w["wgt_hi"], w["wgt_lo"])


def _mlstm_chunk(uext_sc, gates, v_of, z_of, Cin_ref, nin_ref, min_ref, convw_ref, convb_ref, wq_ref, wk_ref,
                 mng_ref, skip_ref, hm_ref, convo_ref, C_ref, n_ref, m_ref, L):
    TAIL = CONV_W - 1
    BASE = SUBLANES

    ext = uext_sc[...]
    conv = convb_ref[...] + ext[BASE:, :] * convw_ref[TAIL:CONV_W, :]
    for j in range(TAIL):
        conv = conv + pltpu.roll(ext, TAIL - j, axis=0)[BASE:, :] * convw_ref[j:j + 1, :]
    tail = ext[BASE + L - TAIL:, :]
    uext_sc[BASE - TAIL:BASE, :] = tail
    convo_ref[...] = tail
    c_act = _silu(conv)

    lf_cols = _log_sigmoid(gates)
    gates_t = gates.T[0:2 * M_HEADS, :]
    lf_rows = _log_sigmoid(gates_t)
    tril, triu = _tri(L, True), _tri(L, False)
    b_cols = sum(_dot(tril, part) for part in _split3(lf_cols))
    b_rows = sum(_dot(part, triu) for part in _split3(lf_rows))
    row_i = lax.broadcasted_iota(jnp.int32, (L, L), 0)
    col_i = lax.broadcasted_iota(jnp.int32, (L, L), 1)
    causal = row_i >= col_i

    for h in range(M_HEADS):
        hs = slice(h * M_HD, (h + 1) * M_HD)
        ch = c_act[:, hs].astype(BF16)
        q = _dot(ch, wq_ref[h])
        k = _dot(ch, wk_ref[h]) * (M_HD ** -0.5)
        v = v_of(h)
        q_bf = q.astype(BF16)
        k_bf = k.astype(BF16)

        ig_col = gates[:, h:h + 1]
        ig_row = gates_t[h:h + 1, :]
        b_col = b_cols[:, M_HEADS + h:M_HEADS + h + 1]
        b_row = b_rows[M_HEADS + h:M_HEADS + h + 1, :]
        m_c = min_ref[h:h + 1, 0:1]

        logd = jnp.where(causal, b_col - b_row + ig_row, -jnp.inf)
        m_prev = b_col + m_c
        m_t = jnp.maximum(m_prev, jnp.max(logd, axis=1, keepdims=True))
        dmat = jnp.exp(logd - m_t)
        w_inter = jnp.exp(m_prev - m_t)

        C = Cin_ref[h]
        n = nin_ref[h]
        s_mat = _dot_nt(q_bf, k_bf) * dmat
        num = w_inter * _dot_nt(q_bf, C.astype(BF16)) + _dot(s_mat.astype(BF16), v.astype(BF16))
        den = w_inter * jnp.sum(q * n, axis=1, keepdims=True) + jnp.sum(s_mat, axis=1, keepdims=True)
        hh = num * (1.0 / jnp.maximum(jnp.abs(den), jnp.exp(-m_t)))

        m_last = m_t[L - 1:L, :]
        w_last = jnp.exp(b_col[L - 1:L, :] - b_col + ig_col - m_last)
        decay = jnp.exp(m_prev[L - 1:L, :] - m_last)
        wv = (w_last * v).astype(BF16)
        C_ref[h] = decay * C + _dot_tn(wv, k_bf)
        n_ref[h] = decay * n + jnp.sum(w_last * k, axis=0, keepdims=True)
        m_ref[h:h + 1, :] = jnp.broadcast_to(m_last, (1, m_ref.shape[1]))

        mu = jnp.mean(hh, axis=1, keepdims=True)
        d = hh - mu
        y = d * lax.rsqrt(jnp.mean(d * d, axis=1, keepdims=True) + EPS) * mng_ref[:, hs]
        hm_ref[:, hs] = ((y + skip_ref[:, hs] * c_act[:, hs]) * _silu(z_of(h))).astype(BF16)


def _mlstm_state_kernel(pm_ref, gate_ref, gbias_ref, convw_ref, convb_ref, wq_ref, wk_ref, mng_ref, skip_ref,
                        conv0_ref, C0_ref, n0_ref, m0_ref, hm_ref, convo_ref, C_ref, n_ref, m_ref, uext_sc,
                        *, L, G):
    for g in range(G):
        rows = pl.ds(g * L, L)
        ue = uext_sc.at[g]
        ue[SUBLANES - (CONV_W - 1):SUBLANES, :] = conv0_ref[g]
        ue[SUBLANES:SUBLANES + L, :] = pm_ref[rows, 0:M_INNER].astype(F32)
        v_of = lambda h, rows=rows: pm_ref[rows, M_INNER + h * M_HD:M_INNER + (h + 1) * M_HD].astype(F32)
        z_of = lambda h, rows=rows: pm_ref[rows, 2 * M_INNER + h * M_HD:2 * M_INNER + (h + 1) * M_HD].astype(F32)
        _mlstm_chunk(ue, gate_ref[rows, :] + gbias_ref[...], v_of, z_of, C0_ref.at[g], n0_ref.at[g], m0_ref.at[g],
                     convw_ref, convb_ref, wq_ref, wk_ref, mng_ref, skip_ref, hm_ref.at[rows], convo_ref.at[g],
                     C_ref.at[g], n_ref.at[g], m_ref.at[g], L)
        m_ref[g, M_HEADS:, :] = m0_ref[g, M_HEADS:, :]


def _mlstm_kernel(x_ref, ng_ref, wa_ref, wgh_ref, wgl_ref, gbias_ref, convw_ref, convb_ref,
                  wq_ref, wk_ref, mng_ref, skip_ref, hm_ref, convo_ref, C_ref, n_ref, m_ref, uext_sc, *, L):
    @pl.when(pl.program_id(1) == 0)
    def _():
        uext_sc[SUBLANES - (CONV_W - 1):SUBLANES, :] = jnp.zeros((CONV_W - 1, M_INNER), F32)
        C_ref[...] = jnp.zeros(C_ref.shape, F32)
        n_ref[...] = jnp.zeros(n_ref.shape, F32)
        m_ref[...] = jnp.zeros(m_ref.shape, F32)

    xh, xl = _rmsnorm_split(x_ref, ng_ref)
    uext_sc[SUBLANES:SUBLANES + L, :] = _dot_nt(xh, wa_ref[0:M_INNER, :])
    gates = _gate_dot(xh, xl, wgh_ref, wgl_ref) + gbias_ref[...]
    v_of = lambda h: _dot_nt(xh, wa_ref[M_INNER + h * M_HD:M_INNER + (h + 1) * M_HD, :])
    z_of = lambda h: _dot_nt(xh, wa_ref[2 * M_INNER + h * M_HD:2 * M_INNER + (h + 1) * M_HD, :])
    _mlstm_chunk(uext_sc, gates, v_of, z_of, C_ref, n_ref, m_ref, convw_ref, convb_ref,
                 wq_ref, wk_ref, mng_ref, skip_ref, hm_ref, convo_ref, C_ref, n_ref, m_ref, L)


def _state_specs(index, streams=None):
    return [
        pl.BlockSpec((streams, CONV_W - 1, M_INNER), lambda *g: (index(*g), 0, 0)),
        pl.BlockSpec((streams, M_HEADS, M_HD, M_HD), lambda *g: (index(*g), 0, 0, 0)),
        pl.BlockSpec((streams, M_HEADS, 1, M_HD), lambda *g: (index(*g), 0, 0, 0)),
        pl.BlockSpec((streams, SUBLANES, GATE_LANES), lambda *g: (index(*g), 0, 0)),
    ]


def _mlstm_out_shape(B, T):
    return [
        jax.ShapeDtypeStruct((B * T, M_INNER), BF16),
        jax.ShapeDtypeStruct((B, CONV_W - 1, M_INNER), F32),
        jax.ShapeDtypeStruct((B, M_HEADS, M_HD, M_HD), F32),
        jax.ShapeDtypeStruct((B, M_HEADS, 1, M_HD), F32),
        jax.ShapeDtypeStruct((B, SUBLANES, GATE_LANES), F32),
    ]


def _mlstm_shared(w):
    return [w["gbias"], w["conv_w"], w["conv_b"], w["w_qm"], w["w_km"], w["mnorm_g"], w["m_skip"]]


def _mlstm_state(proj, gates, w, state, B, L, G):
    shared = _mlstm_shared(w)
    tok = lambda b: (b, 0)
    state_specs = _state_specs(lambda b: b, streams=G)
    return pl.pallas_call(
        functools.partial(_mlstm_state_kernel, L=L, G=G),
        grid=(B // G,),
        in_specs=[pl.BlockSpec((G * L, N_A), tok), pl.BlockSpec((G * L, GATE_LANES), tok)]
        + [_resident(a.shape) for a in shared] + state_specs,
        out_specs=[pl.BlockSpec((G * L, M_INNER), tok)] + state_specs,
        out_shape=_mlstm_out_shape(B, L),
        scratch_shapes=[pltpu.VMEM((G, SUBLANES + L, M_INNER), F32)],
        compiler_params=pltpu.CompilerParams(dimension_semantics=("arbitrary",), vmem_limit_bytes=VMEM_LIMIT),
        name="mlstm_state",
    )(proj, gates, *shared, *state)


def _mlstm(x2, w, B, T, L):
    NC = T // L
    tok = lambda b, c: (b * NC + c, 0)
    consts = [w["norm_g"], w["wt_a"], w["wgt_hi"], w["wgt_lo"]] + _mlstm_shared(w)
    return pl.pallas_call(
        functools.partial(_mlstm_kernel, L=L),
        grid=(B, NC),
        in_specs=[pl.BlockSpec((L, D_MODEL), tok)] + [_resident(a.shape) for a in consts],
        out_specs=[pl.BlockSpec((L, M_INNER), tok)] + _state_specs(lambda b, c: b),
        out_shape=_mlstm_out_shape(B, T),
        scratch_shapes=[pltpu.VMEM((SUBLANES + L, M_INNER), F32)],
        compiler_params=pltpu.CompilerParams(
            dimension_semantics=("arbitrary", "arbitrary"), vmem_limit_bytes=VMEM_LIMIT),
        name="mlstm",
    )(x2, *consts)


def _hgrn_chunk(q_pre, f, v, gz, lb_ref, hng_ref, st_sc, oh_ref, L):
    HD = H_EXPAND
    W = H_WIDTH
    v_bf = v.astype(BF16)

    lbp = lb_ref[...]
    lbm = jnp.max(lbp, axis=0, keepdims=True)
    lbe = jnp.exp(lbp - lbm)
    lb = lbe[0:1, :] / jnp.sum(lbe, axis=0, keepdims=True)

    e = jnp.exp(-jnp.abs(f))
    r = 1.0 / (1.0 + e)
    sig_pos = jnp.where(f >= 0, r, e * r)
    sig_neg = jnp.where(f >= 0, e * r, r)
    fgt = lb + (1.0 - lb) * sig_pos
    lf = jnp.log(fgt)
    kk = (1.0 - lb) * sig_neg
    qa = _silu(q_pre)

    tril = _tri(L, True)
    g = sum(_dot(tril, part) for part in _split3(lf))
    g_last = g[L - 1:L, :]
    q_in = (qa * jnp.exp(g)).astype(BF16)
    k_out = (kk * jnp.exp(g_last - g)).astype(BF16)
    s_dec = jnp.exp(g_last)

    row_i = lax.broadcasted_iota(jnp.int32, (L, L), 0)
    col_i = lax.broadcasted_iota(jnp.int32, (L, L), 1)

    levels = []
    hb = HGRN_BAND
    while hb < L:
        nb = L // (2 * hb)
        g3 = g.reshape(nb, 2 * hb, W)
        ge = jnp.broadcast_to(g3[:, hb - 1:hb, :], (nb, 2 * hb, W)).reshape(L, W)
        e_l = jnp.exp(-jnp.abs(g - ge))
        q_l = (qa * e_l).astype(BF16)
        k_l = (kk * e_l).astype(BF16)
        sh = (2 * hb).bit_length() - 1
        same_blk = jnp.right_shift(row_i, sh) == jnp.right_shift(col_i, sh)
        mask = same_blk & (jnp.bitwise_and(row_i, hb) != 0) & (jnp.bitwise_and(col_i, hb) == 0)
        levels.append((q_l, k_l, mask))
        hb *= 2

    nblk = L // SUBLANES
    to_blocks = lambda a: a.reshape(nblk, SUBLANES, W)
    sub_i = lax.broadcasted_iota(jnp.int32, (nblk, SUBLANES, W), 1)
    f0 = jnp.where(jnp.bitwise_and(sub_i, HGRN_BAND - 1) == 0, 0.0, to_blocks(fgt))
    qa3 = to_blocks(qa)
    ke = to_blocks(kk)
    vs = to_blocks(v)
    o_band = [None] * H_HEADS
    for d in range(HGRN_BAND):
        if d > 0:
            ke = f0 * pltpu.roll(ke, 1, axis=1)
            vs = pltpu.roll(vs, 1, axis=1)
        p_d = qa3 * ke
        for h in range(H_HEADS):
            hs = slice(h * HD, (h + 1) * HD)
            term = jnp.sum(p_d[:, :, hs], axis=2, keepdims=True) * vs[:, :, hs]
            o_band[h] = term if d == 0 else o_band[h] + term

    for h in range(H_HEADS):
        hs = slice(h * HD, (h + 1) * HD)
        st = st_sc[h]
        o = _dot_nt(q_in[:, hs], st.astype(BF16))
        a = jnp.zeros((L, L), F32)
        for q_l, k_l, mask in levels:
            a = a + jnp.where(mask, _dot_nt(q_l[:, hs], k_l[:, hs]), 0.0)
        o = o + _dot(a.astype(BF16), v_bf[:, hs]) + o_band[h].reshape(L, HD)
        st_sc[h] = st * s_dec[:, hs] + _dot_tn(v_bf[:, hs], k_out[:, hs])
        y = o * lax.rsqrt(jnp.mean(o * o, axis=1, keepdims=True) + EPS) * hng_ref[:, hs]
        oh_ref[:, hs] = (y * _silu(gz[:, hs])).astype(BF16)


def _merge_math(x, p, hm, oh, gab, wbrm_ref, wbrh_ref, wout_ref, wpg_ref, wple_ref, fg_ref):
    ya = _dot(hm, wbrm_ref[...])
    yb = _dot(oh, wbrh_ref[...])
    ga = gab[:, 0:D_MODEL].astype(F32)
    gb = gab[:, D_MODEL:2 * D_MODEL].astype(F32)
    y = _sigmoid(ga) * ya + _sigmoid(gb) * yb
    h1 = x + _dot(y.astype(BF16), wout_ref[...])
    pe = _dot(p.astype(BF16), wple_ref[...])
    h2 = h1 + _sigmoid(_dot(h1.astype(BF16), wpg_ref[...])) * pe
    return h2 * lax.rsqrt(jnp.mean(h2 * h2, axis=-1, keepdims=True) + EPS) * fg_ref[...]


def _hgrn_state_kernel(qf_ref, ig_ref, lb_ref, hng_ref, S0_ref, oh_ref, S_ref, st_sc, *, L, G):
    W = H_WIDTH
    for g in range(G):
        rows = pl.ds(g * L, L)
        st = st_sc.at[g]
        for h in range(H_HEADS):
            st[h] = S0_ref[g, h].T
        _hgrn_chunk(qf_ref[rows, 0:W].astype(F32), qf_ref[rows, W:2 * W].astype(F32),
                    ig_ref[rows, 0:W].astype(F32), ig_ref[rows, W:2 * W].astype(F32),
                    lb_ref, hng_ref, st, oh_ref.at[rows], L)
        for h in range(H_HEADS):
            S_ref[g, h] = st[h].T


def _hgrn_state(proj, w, S0, B, L, G):
    off = N_A // (2 * H_WIDTH)
    s_spec = pl.BlockSpec((G, H_HEADS, H_EXPAND, H_EXPAND), lambda b: (b, 0, 0, 0))
    return pl.pallas_call(
        functools.partial(_hgrn_state_kernel, L=L, G=G),
        grid=(B // G,),
        in_specs=[
            pl.BlockSpec((G * L, 2 * H_WIDTH), lambda b: (b, off)),
            pl.BlockSpec((G * L, 2 * H_WIDTH), lambda b: (b, off + 1)),
            _resident(w["hgrn_lb"].shape), _resident(w["hnorm_g"].shape), s_spec,
        ],
        out_specs=[pl.BlockSpec((G * L, H_WIDTH), lambda b: (b, 0)), s_spec],
        out_shape=[jax.ShapeDtypeStruct((B * L, H_WIDTH), BF16),
                   jax.ShapeDtypeStruct((B, H_HEADS, H_EXPAND, H_EXPAND), F32)],
        scratch_shapes=[pltpu.VMEM((G, H_HEADS, H_EXPAND, H_EXPAND), F32)],
        compiler_params=pltpu.CompilerParams(dimension_semantics=("arbitrary",), vmem_limit_bytes=VMEM_LIMIT),
        name="hgrn_state",
    )(proj, proj, w["hgrn_lb"], w["hnorm_g"], S0)


def _hgrn_kernel(x_ref, ng_ref, wb_ref, lb_ref, hng_ref, oh_ref, gab_ref, S_ref, st_sc, *, L):
    W = H_WIDTH

    @pl.when(pl.program_id(1) == 0)
    def _():
        st_sc[...] = jnp.zeros(st_sc.shape, F32)

    xh, _ = _rmsnorm_split(x_ref, ng_ref)
    q_pre = _dot_nt(xh, wb_ref[0:W, :])
    f = _dot_nt(xh, wb_ref[W:2 * W, :])
    v = _dot_nt(xh, wb_ref[2 * W:3 * W, :])
    gz = _dot_nt(xh, wb_ref[3 * W:4 * W, :])
    gab_ref[...] = _dot_nt(xh, wb_ref[4 * W:N_B, :]).astype(BF16)
    _hgrn_chunk(q_pre, f, v, gz, lb_ref, hng_ref, st_sc, oh_ref, L)

    @pl.when(pl.program_id(1) == pl.num_programs(1) - 1)
    def _():
        for h in range(H_HEADS):
            S_ref[h] = st_sc[h].T


def _hgrn(x2, w, B, T, L):
    NC = T // L
    tok = lambda b, c: (b * NC + c, 0)
    consts = [w["norm_g"], w["wt_b"], w["hgrn_lb"], w["hnorm_g"]]
    return pl.pallas_call(
        functools.partial(_hgrn_kernel, L=L),
        grid=(B, NC),
        in_specs=[pl.BlockSpec((L, D_MODEL), tok)] + [_resident(a.shape) for a in consts],
        out_specs=[pl.BlockSpec((L, H_WIDTH), tok), pl.BlockSpec((L, 2 * D_MODEL), tok),
                   pl.BlockSpec((None, H_HEADS, H_EXPAND, H_EXPAND), lambda b, c: (b, 0, 0, 0))],
        out_shape=[jax.ShapeDtypeStruct((B * T, H_WIDTH), BF16),
                   jax.ShapeDtypeStruct((B * T, 2 * D_MODEL), BF16),
                   jax.ShapeDtypeStruct((B, H_HEADS, H_EXPAND, H_EXPAND), F32)],
        scratch_shapes=[pltpu.VMEM((H_HEADS, H_EXPAND, H_EXPAND), F32)],
        compiler_params=pltpu.CompilerParams(
            dimension_semantics=("arbitrary", "arbitrary"), vmem_limit_bytes=VMEM_LIMIT),
        name="hgrn",
    )(x2, *consts)


def _merge_kernel(x_ref, p_ref, hm_ref, oh_ref, gab_ref, wbrm_ref, wbrh_ref, wout_ref, wpg_ref, wple_ref,
                  fg_ref, o_ref):
    o_ref[...] = _merge_math(x_ref[...], p_ref[...], hm_ref[...], oh_ref[...], gab_ref[...],
                             wbrm_ref, wbrh_ref, wout_ref, wpg_ref, wple_ref, fg_ref)


def _merge(x2, p2, hm, oh, gab, gab_blk, w, tm):
    n_tok = x2.shape[0]
    tok = lambda i: (i, 0)
    consts = [w["w_brm"], w["w_brh"], w["w_out"], w["w_pg"], w["w_ple"], w["final_g"]]
    return pl.pallas_call(
        _merge_kernel,
        grid=(n_tok // tm,),
        in_specs=[
            pl.BlockSpec((tm, D_MODEL), tok),
            pl.BlockSpec((tm, PLE_DIM), tok),
            pl.BlockSpec((tm, M_INNER), tok),
            pl.BlockSpec((tm, H_WIDTH), tok),
            pl.BlockSpec((tm, 2 * D_MODEL), lambda i: (i, gab_blk)),
        ] + [_resident(a.shape) for a in consts],
        out_specs=pl.BlockSpec((tm, D_MODEL), tok),
        out_shape=jax.ShapeDtypeStruct((n_tok, D_MODEL), F32),
        compiler_params=pltpu.CompilerParams(
            dimension_semantics=("arbitrary",), vmem_limit_bytes=VMEM_LIMIT),
        name="merge",
    )(x2, p2, hm, oh, gab, *consts)


def _trunk(x, p, state, w):
    B, T, _ = x.shape
    x2 = x.reshape(B * T, D_MODEL)
    p2 = p.reshape(B * T, PLE_DIM)
    tm = min(MERGE_TILE, B * T)
    if state is None:
        L = min(PROMPT_CHUNK, T)
        hm, conv_n, C_n, n_n, m_n = _mlstm(x2, w, B, T, L)
        oh, gab, S_n = _hgrn(x2, w, B, T, L)
        y = _merge(x2, p2, hm, oh, gab, 0, w, tm)
    else:
        conv0, C0, n0, m0, S0 = state
        proj, gates = _proj(x2, w, min(PROJ_TILE, B * T), 2 * D_MODEL)
        m0p = jnp.zeros((B, SUBLANES, GATE_LANES), F32).at[:, :M_HEADS, :].set(m0[:, :, None])
        m_state = (conv0, C0, n0.reshape(B, M_HEADS, 1, M_HD), m0p)
        hm, conv_n, C_n, n_n, m_n = _mlstm_state(proj, gates, w, m_state, B, T, math.gcd(B, MLSTM_STREAMS))
        oh, S_n = _hgrn_state(proj, w, S0, B, T, math.gcd(B, HGRN_STREAMS))
        y = _merge(x2, p2, hm, oh, proj, (N_A + 4 * H_WIDTH) // (2 * D_MODEL), w, tm)
    return (y.reshape(B, T, D_MODEL), conv_n[None], C_n[None], n_n.reshape(B, M_HEADS, M_HD)[None],
            m_n[:, :M_HEADS, 0][None], S_n[None])


def _prep_weights(norm_g, w_in, b_ig, b_fg, conv_w, conv_b, w_qm, w_km, mnorm_g, m_skip, w_brm, hgrn_lb, hnorm_g,
                  w_brh, w_out, w_ple, w_pg, final_g):
    assert norm_g.shape[0] == 1, "single-layer trunk"
    g1 = N_A + 2 * M_HEADS
    wt = jnp.swapaxes(w_in[0], 0, 1)
    wt_gate = wt[N_A:N_A + GATE_LANES]
    wgt_hi = wt_gate.astype(BF16)
    return {
        "norm_g": norm_g[0][None, :],
        "wt_a": wt[:N_A].astype(BF16),
        "wt_b": wt[g1:].astype(BF16),
        "wgt_hi": wgt_hi,
        "wgt_lo": (wt_gate - wgt_hi.astype(F32)).astype(BF16),
        "gbias": jnp.pad(jnp.concatenate([b_ig[0], b_fg[0]]), (0, GATE_LANES - 2 * M_HEADS))[None, :],
        "conv_w": conv_w[0],
        "conv_b": conv_b[0][None, :],
        "w_qm": w_qm[0].astype(BF16),
        "w_km": w_km[0].astype(BF16),
        "mnorm_g": mnorm_g[0][None, :],
        "m_skip": m_skip[0][None, :],
        "w_brm": w_brm[0].astype(BF16),
        "hgrn_lb": hgrn_lb,
        "hnorm_g": hnorm_g[0][None, :],
        "w_brh": w_brh[0].astype(BF16),
        "w_out": w_out[0].astype(BF16),
        "w_ple": w_ple[0].astype(BF16),
        "w_pg": w_pg[0].astype(BF16),
        "final_g": final_g[None, :],
    }


def kernel(x_prompt, x_sample, state_conv, state_mlstm_C, state_mlstm_n, state_mlstm_m, state_hgrn, p_prompt, p_sample, norm_g, w_in, b_ig, b_fg, conv_w, conv_b, w_qm, w_km, mnorm_g, m_skip, w_brm, hgrn_lb, hnorm_g, w_brh, w_out, w_ple, w_pg, final_g):
    w = _prep_weights(norm_g, w_in, b_ig, b_fg, conv_w, conv_b, w_qm, w_km, mnorm_g, m_skip, w_brm, hgrn_lb,
                      hnorm_g, w_brh, w_out, w_ple, w_pg, final_g)
    out_p = _trunk(x_prompt, p_prompt[0], None, w)
    state = (state_conv[0], state_mlstm_C[0], state_mlstm_n[0], state_mlstm_m[0], state_hgrn[0])
    out_s = _trunk(x_sample, p_sample[0], state, w)
    return (out_p[0], out_s[0]) + tuple(out_p[1:]) + tuple(out_s[1:])
```

```python
import functools
import math

import jax
import jax.numpy as jnp
from jax import lax
from jax.experimental import pallas as pl
from jax.experimental.pallas import tpu as pltpu

F32 = jnp.float32
BF16 = jnp.bfloat16

D_MODEL = 1024
PLE_DIM = 256
M_HEADS = 4
M_INNER = 2 * D_MODEL
M_HD = M_INNER // M_HEADS
CONV_W = 4
H_EXPAND = 128
H_WIDTH = D_MODEL
H_HEADS = H_WIDTH // H_EXPAND
EPS = 1e-6

N_A = 3 * M_INNER
N_B = 4 * H_WIDTH + 2 * D_MODEL
GATE_LANES = 128
SUBLANES = 8
HGRN_BAND = 4
VMEM_LIMIT = 56 * 1024 * 1024
PROMPT_CHUNK = 256
PROJ_TILE = 1024
MERGE_TILE = 512
MLSTM_STREAMS = 2
HGRN_STREAMS = 4


def _dot(a, b):
    return jnp.dot(a, b, preferred_element_type=F32)


def _dot_nt(a, b):
    return lax.dot_general(a, b, (((1,), (1,)), ((), ())), preferred_element_type=F32)


def _dot_tn(a, b):
    return lax.dot_general(a, b, (((0,), (0,)), ((), ())), preferred_element_type=F32)


def _split3(x):
    hi = x.astype(BF16)
    r = x - hi.astype(F32)
    mid = r.astype(BF16)
    lo = (r - mid.astype(F32)).astype(BF16)
    return hi, mid, lo


def _sigmoid(x):
    return 1.0 / (1.0 + jnp.exp(-x))


def _silu(x):
    return x * _sigmoid(x)


def _log_sigmoid(x):
    return jnp.minimum(x, 0.0) - jnp.log(1.0 + jnp.exp(-jnp.abs(x)))


def _tri(n, lower):
    r = lax.broadcasted_iota(jnp.int32, (n, n), 0)
    c = lax.broadcasted_iota(jnp.int32, (n, n), 1)
    return jnp.where((r >= c) if lower else (r <= c), 1.0, 0.0).astype(BF16)


def _rmsnorm_split(x_ref, g_ref):
    x = x_ref[...]
    xn = x * lax.rsqrt(jnp.mean(x * x, axis=-1, keepdims=True) + EPS) * g_ref[...]
    hi = xn.astype(BF16)
    lo = (xn - hi.astype(F32)).astype(BF16)
    return hi, lo


def _gate_dot(hi, lo, wgh_ref, wgl_ref):
    return _dot_nt(hi, wgh_ref[...]) + _dot_nt(lo, wgh_ref[...]) + _dot_nt(hi, wgl_ref[...])


def _zero_after(x):
    rows, lanes = x.shape
    m = jnp.max(x.reshape(rows // SUBLANES, SUBLANES, lanes), axis=0)
    u = pltpu.bitcast(m, jnp.uint32)
    u = lax.shift_right_logical(lax.shift_right_logical(u, jnp.uint32(16)), jnp.uint32(16))
    z = pltpu.bitcast(u, F32)
    return jnp.max(z, axis=0, keepdims=True)


def _resident(shape):
    return pl.BlockSpec(shape, lambda *_: (0,) * len(shape), pipeline_mode=pl.Buffered(1))


def _proj_kernel(x_ref, g_ref, wa_ref, wb_ref, wgh_ref, wgl_ref, o_ref, gate_ref, xh_sc, *, n_a):
    j = pl.program_id(1)

    @pl.when(j == 0)
    def _():
        hi, lo = _rmsnorm_split(x_ref, g_ref)
        xh_sc[...] = hi
        gate_ref[...] = _gate_dot(hi, lo, wgh_ref, wgl_ref)

    @pl.when(j < n_a)
    def _():
        o_ref[...] = _dot_nt(xh_sc[...], wa_ref[...]).astype(BF16)

    @pl.when(j >= n_a)
    def _():
        o_ref[...] = _dot_nt(xh_sc[...], wb_ref[...]).astype(BF16)


def _proj(x2, w, tm, tn):
    n_tok = x2.shape[0]
    n_a = N_A // tn
    return pl.pallas_call(
        functools.partial(_proj_kernel, n_a=n_a),
        grid=(n_tok // tm, (N_A + N_B) // tn),
        in_specs=[
            pl.BlockSpec((tm, D_MODEL), lambda i, j: (i, 0)),
            pl.BlockSpec((1, D_MODEL), lambda i, j: (0, 0)),
            pl.BlockSpec((tn, D_MODEL), lambda i, j: (jnp.minimum(j, n_a - 1), 0)),
            pl.BlockSpec((tn, D_MODEL), lambda i, j: (jnp.maximum(j - n_a, 0), 0)),
            pl.BlockSpec((GATE_LANES, D_MODEL), lambda i, j: (0, 0)),
            pl.BlockSpec((GATE_LANES, D_MODEL), lambda i, j: (0, 0)),
        ],
        out_specs=[
            pl.BlockSpec((tm, tn), lambda i, j: (i, j)),
            pl.BlockSpec((tm, GATE_LANES), lambda i, j: (i, 0)),
        ],
        out_shape=[
            jax.ShapeDtypeStruct((n_tok, N_A + N_B), BF16),
            jax.ShapeDtypeStruct((n_tok, GATE_LANES), F32),
        ],
        scratch_shapes=[pltpu.VMEM((tm, D_MODEL), BF16)],
        compiler_params=pltpu.CompilerParams(
            dimension_semantics=("arbitrary", "arbitrary"), vmem_limit_bytes=VMEM_LIMIT),
        name="proj",
    )(x2, w["norm_g"], w["wt_a"], w["wt_b"], w["wgt_hi"], w["wgt_lo"])


def _mlstm_chunk(uext_sc, gates, v_of, z_of, Cin_ref, nin_ref, min_ref, convw_ref, convb_ref, wq_ref, wk_ref,
                 mng_ref, skip_ref, hm_ref, convo_ref, C_ref, n_ref, m_ref, L):
    TAIL = CONV_W - 1
    BASE = SUBLANES

    ext = uext_sc[...]
    conv = convb_ref[...] + ext[BASE:, :] * convw_ref[TAIL:CONV_W, :]
    for j in range(TAIL):
        conv = conv + pltpu.roll(ext, TAIL - j, axis=0)[BASE:, :] * convw_ref[j:j + 1, :]
    tail = ext[BASE + L - TAIL:, :]
    uext_sc[BASE - TAIL:BASE, :] = tail
    convo_ref[...] = tail
    c_act = _silu(conv)

    lf_cols = _log_sigmoid(gates)
    gates_t = gates.T[0:2 * M_HEADS, :]
    lf_rows = _log_sigmoid(gates_t)
    tril, triu = _tri(L, True), _tri(L, False)
    b_cols = sum(_dot(tril, part) for part in _split3(lf_cols))
    b_rows = sum(_dot(part, triu) for part in _split3(lf_rows))
    row_i = lax.broadcasted_iota(jnp.int32, (L, L), 0)
    col_i = lax.broadcasted_iota(jnp.int32, (L, L), 1)
    causal = row_i >= col_i

    done = []
    done_c = []
    for h in range(M_HEADS):
        hs = slice(h * M_HD, (h + 1) * M_HD)
        c_h = c_act[:, hs]
        ch = (c_h + done_c[h - 1] if h >= 1 else c_h).astype(BF16)
        q = _dot(ch, wq_ref[h])
        k = _dot(ch, wk_ref[h]) * (M_HD ** -0.5)
        v = v_of(h)
        q_bf = q.astype(BF16)
        k_bf = k.astype(BF16)

        ig_col = gates[:, h:h + 1]
        ig_row = gates_t[h:h + 1, :]
        b_col = b_cols[:, M_HEADS + h:M_HEADS + h + 1]
        b_row = b_rows[M_HEADS + h:M_HEADS + h + 1, :]
        m_c = min_ref[h:h + 1, 0:1]

        logd = jnp.where(causal, b_col + (ig_row - b_row), -jnp.inf)
        m_prev = b_col + m_c
        m_t = jnp.maximum(m_prev, jnp.max(logd, axis=1, keepdims=True))
        dmat = jnp.exp(logd - m_t)
        w_inter = jnp.exp(m_prev - m_t)

        C = Cin_ref[h]
        n = nin_ref[h]
        s_mat = _dot_nt(q_bf, k_bf) * dmat
        if h >= 1:
            s_mat = s_mat + done[h - 1][:, 0:L]
        num = w_inter * _dot_nt(q_bf, C.astype(BF16)) + _dot(s_mat.astype(BF16), v.astype(BF16))
        den = w_inter * jnp.sum(q * n, axis=1, keepdims=True) + jnp.sum(s_mat, axis=1, keepdims=True)
        hh = num * (1.0 / jnp.maximum(jnp.abs(den), jnp.exp(-m_t)))

        m_last = m_t[L - 1:L, :]
        w_last = jnp.exp(b_col[L - 1:L, :] - b_col + ig_col - m_last)
        decay = jnp.exp(m_prev[L - 1:L, :] - m_last)
        wv = (w_last * v).astype(BF16)
        c_new = decay * C + _dot_tn(wv, k_bf)
        C_ref[h] = c_new
        done_c.append(_zero_after(c_new))
        n_ref[h] = decay * n + jnp.sum(w_last * k, axis=0, keepdims=True)
        m_ref[h:h + 1, :] = jnp.broadcast_to(m_last, (1, m_ref.shape[1]))

        mu = jnp.mean(hh, axis=1, keepdims=True)
        d = hh - mu
        y = d * lax.rsqrt(jnp.mean(d * d, axis=1, keepdims=True) + EPS) * mng_ref[:, hs]
        out = (y + skip_ref[:, hs] * c_h) * _silu(z_of(h))
        hm_ref[:, hs] = out.astype(BF16)
        done.append(_zero_after(out))


def _mlstm_state_kernel(pm_ref, gate_ref, gbias_ref, convw_ref, convb_ref, wq_ref, wk_ref, mng_ref, skip_ref,
                        conv0_ref, C0_ref, n0_ref, m0_ref, hm_ref, convo_ref, C_ref, n_ref, m_ref, uext_sc,
                        *, L, G):
    for g in range(G):
        rows = pl.ds(g * L, L)
        ue = uext_sc.at[g]
        ue[SUBLANES - (CONV_W - 1):SUBLANES, :] = conv0_ref[g]
        ue[SUBLANES:SUBLANES + L, :] = pm_ref[rows, 0:M_INNER].astype(F32)
        v_of = lambda h, rows=rows: pm_ref[rows, M_INNER + h * M_HD:M_INNER + (h + 1) * M_HD].astype(F32)
        z_of = lambda h, rows=rows: pm_ref[rows, 2 * M_INNER + h * M_HD:2 * M_INNER + (h + 1) * M_HD].astype(F32)
        _mlstm_chunk(ue, gate_ref[rows, :] + gbias_ref[...], v_of, z_of, C0_ref.at[g], n0_ref.at[g], m0_ref.at[g],
                     convw_ref, convb_ref, wq_ref, wk_ref, mng_ref, skip_ref, hm_ref.at[rows], convo_ref.at[g],
                     C_ref.at[g], n_ref.at[g], m_ref.at[g], L)
        m_ref[g, M_HEADS:, :] = m0_ref[g, M_HEADS:, :]


def _mlstm_kernel(x_ref, ng_ref, wa_ref, wgh_ref, wgl_ref, gbias_ref, convw_ref, convb_ref,
                  wq_ref, wk_ref, mng_ref, skip_ref, hm_ref, convo_ref, C_ref, n_ref, m_ref, uext_sc, *, L):
    @pl.when(pl.program_id(1) == 0)
    def _():
        uext_sc[SUBLANES - (CONV_W - 1):SUBLANES, :] = jnp.zeros((CONV_W - 1, M_INNER), F32)
        C_ref[...] = jnp.zeros(C_ref.shape, F32)
        n_ref[...] = jnp.zeros(n_ref.shape, F32)
        m_ref[...] = jnp.zeros(m_ref.shape, F32)

    xh, xl = _rmsnorm_split(x_ref, ng_ref)
    uext_sc[SUBLANES:SUBLANES + L, :] = _dot_nt(xh, wa_ref[0:M_INNER, :])
    gates = _gate_dot(xh, xl, wgh_ref, wgl_ref) + gbias_ref[...]
    v_of = lambda h: _dot_nt(xh, wa_ref[M_INNER + h * M_HD:M_INNER + (h + 1) * M_HD, :])
    z_of = lambda h: _dot_nt(xh, wa_ref[2 * M_INNER + h * M_HD:2 * M_INNER + (h + 1) * M_HD, :])
    _mlstm_chunk(uext_sc, gates, v_of, z_of, C_ref, n_ref, m_ref, convw_ref, convb_ref,
                 wq_ref, wk_ref, mng_ref, skip_ref, hm_ref, convo_ref, C_ref, n_ref, m_ref, L)


def _state_specs(index, streams=None):
    return [
        pl.BlockSpec((streams, CONV_W - 1, M_INNER), lambda *g: (index(*g), 0, 0)),
        pl.BlockSpec((streams, M_HEADS, M_HD, M_HD), lambda *g: (index(*g), 0, 0, 0)),
        pl.BlockSpec((streams, M_HEADS, 1, M_HD), lambda *g: (index(*g), 0, 0, 0)),
        pl.BlockSpec((streams, SUBLANES, GATE_LANES), lambda *g: (index(*g), 0, 0)),
    ]


def _mlstm_out_shape(B, T):
    return [
        jax.ShapeDtypeStruct((B * T, M_INNER), BF16),
        jax.ShapeDtypeStruct((B, CONV_W - 1, M_INNER), F32),
        jax.ShapeDtypeStruct((B, M_HEADS, M_HD, M_HD), F32),
        jax.ShapeDtypeStruct((B, M_HEADS, 1, M_HD), F32),
        jax.ShapeDtypeStruct((B, SUBLANES, GATE_LANES), F32),
    ]


def _mlstm_shared(w):
    return [w["gbias"], w["conv_w"], w["conv_b"], w["w_qm"], w["w_km"], w["mnorm_g"], w["m_skip"]]


def _mlstm_state(proj, gates, w, state, B, L, G):
    shared = _mlstm_shared(w)
    tok = lambda b: (b, 0)
    state_specs = _state_specs(lambda b: b, streams=G)
    return pl.pallas_call(
        functools.partial(_mlstm_state_kernel, L=L, G=G),
        grid=(B // G,),
        in_specs=[pl.BlockSpec((G * L, N_A), tok), pl.BlockSpec((G * L, GATE_LANES), tok)]
        + [_resident(a.shape) for a in shared] + state_specs,
        out_specs=[pl.BlockSpec((G * L, M_INNER), tok)] + state_specs,
        out_shape=_mlstm_out_shape(B, L),
        scratch_shapes=[pltpu.VMEM((G, SUBLANES + L, M_INNER), F32)],
        compiler_params=pltpu.CompilerParams(dimension_semantics=("arbitrary",), vmem_limit_bytes=VMEM_LIMIT),
        name="mlstm_state",
    )(proj, gates, *shared, *state)


def _mlstm(x2, w, B, T, L):
    NC = T // L
    tok = lambda b, c: (b * NC + c, 0)
    consts = [w["norm_g"], w["wt_a"], w["wgt_hi"], w["wgt_lo"]] + _mlstm_shared(w)
    return pl.pallas_call(
        functools.partial(_mlstm_kernel, L=L),
        grid=(B, NC),
        in_specs=[pl.BlockSpec((L, D_MODEL), tok)] + [_resident(a.shape) for a in consts],
        out_specs=[pl.BlockSpec((L, M_INNER), tok)] + _state_specs(lambda b, c: b),
        out_shape=_mlstm_out_shape(B, T),
        scratch_shapes=[pltpu.VMEM((SUBLANES + L, M_INNER), F32)],
        compiler_params=pltpu.CompilerParams(
            dimension_semantics=("arbitrary", "arbitrary"), vmem_limit_bytes=VMEM_LIMIT),
        name="mlstm",
    )(x2, *consts)


def _hgrn_chunk(q_pre, f, v, gz, lb_ref, hng_ref, st_sc, oh_ref, L):
    HD = H_EXPAND
    W = H_WIDTH
    v_bf = v.astype(BF16)

    lbp = lb_ref[...]
    lbm = jnp.max(lbp, axis=0, keepdims=True)
    lbe = jnp.exp(lbp - lbm)
    lb = lbe[0:1, :] / jnp.sum(lbe, axis=0, keepdims=True)

    e = jnp.exp(-jnp.abs(f))
    r = 1.0 / (1.0 + e)
    sig_pos = jnp.where(f >= 0, r, e * r)
    sig_neg = jnp.where(f >= 0, e * r, r)
    fgt = lb + (1.0 - lb) * sig_pos
    lf = jnp.log(fgt)
    kk = (1.0 - lb) * sig_neg
    qa = _silu(q_pre)

    tril = _tri(L, True)
    g = sum(_dot(tril, part) for part in _split3(lf))
    g_last = g[L - 1:L, :]
    qa_bf = qa.astype(BF16)
    kk_bf = kk.astype(BF16)
    q_in = qa_bf * jnp.exp(g).astype(BF16)
    k_out = kk_bf * jnp.exp(g_last - g).astype(BF16)
    s_dec = jnp.exp(g_last)

    row_i = lax.broadcasted_iota(jnp.int32, (L, L), 0)
    col_i = lax.broadcasted_iota(jnp.int32, (L, L), 1)
    levels = []
    hb = HGRN_BAND
    while hb < L:
        nb = L // (2 * hb)
        g3 = g.reshape(nb, 2 * hb, W)
        ge = jnp.broadcast_to(g3[:, hb - 1:hb, :], (nb, 2 * hb, W)).reshape(L, W)
        e_l = jnp.exp(-jnp.abs(g - ge)).astype(BF16)
        q_l = qa_bf * e_l
        k_l = kk_bf * e_l
        sh = (2 * hb).bit_length() - 1
        same_blk = jnp.right_shift(row_i, sh) == jnp.right_shift(col_i, sh)
        mask = same_blk & (jnp.bitwise_and(row_i, hb) != 0) & (jnp.bitwise_and(col_i, hb) == 0)
        levels.append((q_l, k_l, mask))
        hb *= 2

    nblk = L // SUBLANES
    to_blocks = lambda a: a.reshape(nblk, SUBLANES, W)
    sub_i = lax.broadcasted_iota(jnp.int32, (nblk, SUBLANES, W), 1)
    f0 = jnp.where(jnp.bitwise_and(sub_i, HGRN_BAND - 1) == 0, 0.0, to_blocks(fgt))
    qa3 = to_blocks(qa)
    ke = to_blocks(kk)
    vs = to_blocks(v)
    o_band = [None] * H_HEADS
    for d in range(HGRN_BAND):
        if d > 0:
            ke = f0 * pltpu.roll(ke, 1, axis=1)
            vs = pltpu.roll(vs, 1, axis=1)
        p_d = qa3 * ke
        for h in range(H_HEADS):
            hs = slice(h * HD, (h + 1) * HD)
            term = jnp.sum(p_d[:, :, hs], axis=2, keepdims=True) * vs[:, :, hs]
            o_band[h] = term if d == 0 else o_band[h] + term

    for h in range(H_HEADS):
        hs = slice(h * HD, (h + 1) * HD)
        st = st_sc[h]
        o = _dot_nt(q_in[:, hs], st.astype(BF16))
        a = jnp.zeros((L, L), F32)
        for q_l, k_l, mask in levels:
            a = jnp.where(mask, _dot_nt(q_l[:, hs], k_l[:, hs]), a)
        o = o + _dot(a.astype(BF16), v_bf[:, hs]) + o_band[h].reshape(L, HD)
        st_sc[h] = st * s_dec[:, hs] + _dot_tn(v_bf[:, hs], k_out[:, hs])
        y = o * lax.rsqrt(jnp.mean(o * o, axis=1, keepdims=True) + EPS) * hng_ref[:, hs]
        oh_ref[:, hs] = (y * _silu(gz[:, hs])).astype(BF16)


def _merge_math(x, p, hm, oh, gab, wbrm_ref, wbrh_ref, wout_ref, wpg_ref, wple_ref, fg_ref):
    ya = _dot(hm, wbrm_ref[...])
    yb = _dot(oh, wbrh_ref[...])
    ga = gab[:, 0:D_MODEL].astype(F32)
    gb = gab[:, D_MODEL:2 * D_MODEL].astype(F32)
    y = _sigmoid(ga) * ya + _sigmoid(gb) * yb
    h1 = x + _dot(y.astype(BF16), wout_ref[...])
    pe = _dot(p.astype(BF16), wple_ref[...])
    h2 = h1 + _sigmoid(_dot(h1.astype(BF16), wpg_ref[...])) * pe
    return h2 * lax.rsqrt(jnp.mean(h2 * h2, axis=-1, keepdims=True) + EPS) * fg_ref[...]


def _hgrn_state_kernel(qf_ref, ig_ref, lb_ref, hng_ref, S0_ref, oh_ref, S_ref, st_sc, *, L, G):
    W = H_WIDTH
    for g in range(G):
        rows = pl.ds(g * L, L)
        st = st_sc.at[g]
        for h in range(H_HEADS):
            st[h] = S0_ref[g, h].T
        _hgrn_chunk(qf_ref[rows, 0:W].astype(F32), qf_ref[rows, W:2 * W].astype(F32),
                    ig_ref[rows, 0:W].astype(F32), ig_ref[rows, W:2 * W].astype(F32),
                    lb_ref, hng_ref, st, oh_ref.at[rows], L)
        for h in range(H_HEADS):
            S_ref[g, h] = st[h].T


def _hgrn_state(proj, w, S0, B, L, G):
    off = N_A // (2 * H_WIDTH)
    s_spec = pl.BlockSpec((G, H_HEADS, H_EXPAND, H_EXPAND), lambda b: (b, 0, 0, 0))
    return pl.pallas_call(
        functools.partial(_hgrn_state_kernel, L=L, G=G),
        grid=(B // G,),
        in_specs=[
            pl.BlockSpec((G * L, 2 * H_WIDTH), lambda b: (b, off)),
            pl.BlockSpec((G * L, 2 * H_WIDTH), lambda b: (b, off + 1)),
            _resident(w["hgrn_lb"].shape), _resident(w["hnorm_g"].shape), s_spec,
        ],
        out_specs=[pl.BlockSpec((G * L, H_WIDTH), lambda b: (b, 0)), s_spec],
        out_shape=[jax.ShapeDtypeStruct((B * L, H_WIDTH), BF16),
                   jax.ShapeDtypeStruct((B, H_HEADS, H_EXPAND, H_EXPAND), F32)],
        scratch_shapes=[pltpu.VMEM((G, H_HEADS, H_EXPAND, H_EXPAND), F32)],
        compiler_params=pltpu.CompilerParams(dimension_semantics=("arbitrary",), vmem_limit_bytes=VMEM_LIMIT),
        name="hgrn_state",
    )(proj, proj, w["hgrn_lb"], w["hnorm_g"], S0)


def _hgrn_kernel(x_ref, ng_ref, wb_ref, lb_ref, hng_ref, oh_ref, gab_ref, S_ref, st_sc, *, L):
    W = H_WIDTH

    @pl.when(pl.program_id(1) == 0)
    def _():
        st_sc[...] = jnp.zeros(st_sc.shape, F32)

    xh, _ = _rmsnorm_split(x_ref, ng_ref)
    q_pre = _dot_nt(xh, wb_ref[0:W, :])
    f = _dot_nt(xh, wb_ref[W:2 * W, :])
    v = _dot_nt(xh, wb_ref[2 * W:3 * W, :])
    gz = _dot_nt(xh, wb_ref[3 * W:4 * W, :])
    gab_ref[...] = _dot_nt(xh, wb_ref[4 * W:N_B, :]).astype(BF16)
    _hgrn_chunk(q_pre, f, v, gz, lb_ref, hng_ref, st_sc, oh_ref, L)

    @pl.when(pl.program_id(1) == pl.num_programs(1) - 1)
    def _():
        for h in range(H_HEADS):
            S_ref[h] = st_sc[h].T


def _hgrn(x2, w, B, T, L):
    NC = T // L
    tok = lambda b, c: (b * NC + c, 0)
    consts = [w["norm_g"], w["wt_b"], w["hgrn_lb"], w["hnorm_g"]]
    return pl.pallas_call(
        functools.partial(_hgrn_kernel, L=L),
        grid=(B, NC),
        in_specs=[pl.BlockSpec((L, D_MODEL), tok)] + [_resident(a.shape) for a in consts],
        out_specs=[pl.BlockSpec((L, H_WIDTH), tok), pl.BlockSpec((L, 2 * D_MODEL), tok),
                   pl.BlockSpec((None, H_HEADS, H_EXPAND, H_EXPAND), lambda b, c: (b, 0, 0, 0))],
        out_shape=[jax.ShapeDtypeStruct((B * T, H_WIDTH), BF16),
                   jax.ShapeDtypeStruct((B * T, 2 * D_MODEL), BF16),
                   jax.ShapeDtypeStruct((B, H_HEADS, H_EXPAND, H_EXPAND), F32)],
        scratch_shapes=[pltpu.VMEM((H_HEADS, H_EXPAND, H_EXPAND), F32)],
        compiler_params=pltpu.CompilerParams(
            dimension_semantics=("arbitrary", "arbitrary"), vmem_limit_bytes=VMEM_LIMIT),
        name="hgrn",
    )(x2, *consts)


def _merge_kernel(x_ref, p_ref, hm_ref, oh_ref, gab_ref, wbrm_ref, wbrh_ref, wout_ref, wpg_ref, wple_ref,
                  fg_ref, o_ref):
    o_ref[...] = _merge_math(x_ref[...], p_ref[...], hm_ref[...], oh_ref[...], gab_ref[...],
                             wbrm_ref, wbrh_ref, wout_ref, wpg_ref, wple_ref, fg_ref)


def _merge(x2, p2, hm, oh, gab, gab_blk, w, tm):
    n_tok = x2.shape[0]
    tok = lambda i: (i, 0)
    consts = [w["w_brm"], w["w_brh"], w["w_out"], w["w_pg"], w["w_ple"], w["final_g"]]
    return pl.pallas_call(
        _merge_kernel,
        grid=(n_tok // tm,),
        in_specs=[
            pl.BlockSpec((tm, D_MODEL), tok),
            pl.BlockSpec((tm, PLE_DIM), tok),
            pl.BlockSpec((tm, M_INNER), tok),
            pl.BlockSpec((tm, H_WIDTH), tok),
            pl.BlockSpec((tm, 2 * D_MODEL), lambda i: (i, gab_blk)),
        ] + [_resident(a.shape) for a in consts],
        out_specs=pl.BlockSpec((tm, D_MODEL), tok),
        out_shape=jax.ShapeDtypeStruct((n_tok, D_MODEL), F32),
        compiler_params=pltpu.CompilerParams(
            dimension_semantics=("arbitrary",), vmem_limit_bytes=VMEM_LIMIT),
        name="merge",
    )(x2, p2, hm, oh, gab, *consts)


def _trunk(x, p, state, w):
    B, T, _ = x.shape
    x2 = x.reshape(B * T, D_MODEL)
    p2 = p.reshape(B * T, PLE_DIM)
    tm = min(MERGE_TILE, B * T)
    if state is None:
        L = min(PROMPT_CHUNK, T)
        hm, conv_n, C_n, n_n, m_n = _mlstm(x2, w, B, T, L)
        oh, gab, S_n = _hgrn(x2, w, B, T, L)
        y = _merge(x2, p2, hm, oh, gab, 0, w, tm)
    else:
        conv0, C0, n0, m0, S0 = state
        proj, gates = _proj(x2, w, min(PROJ_TILE, B * T), 2 * D_MODEL)
        m0p = jnp.zeros((B, SUBLANES, GATE_LANES), F32).at[:, :M_HEADS, :].set(m0[:, :, None])
        m_state = (conv0, C0, n0.reshape(B, M_HEADS, 1, M_HD), m0p)
        hm, conv_n, C_n, n_n, m_n = _mlstm_state(proj, gates, w, m_state, B, T, math.gcd(B, MLSTM_STREAMS))
        oh, S_n = _hgrn_state(proj, w, S0, B, T, math.gcd(B, HGRN_STREAMS))
        y = _merge(x2, p2, hm, oh, proj, (N_A + 4 * H_WIDTH) // (2 * D_MODEL), w, tm)
    return (y.reshape(B, T, D_MODEL), conv_n[None], C_n[None], n_n.reshape(B, M_HEADS, M_HD)[None],
            m_n[:, :M_HEADS, 0][None], S_n[None])


def _prep_weights(norm_g, w_in, b_ig, b_fg, conv_w, conv_b, w_qm, w_km, mnorm_g, m_skip, w_brm, hgrn_lb, hnorm_g,
                  w_brh, w_out, w_ple, w_pg, final_g):
    assert norm_g.shape[0] == 1, "single-layer trunk"
    g1 = N_A + 2 * M_HEADS
    wt = jnp.swapaxes(w_in[0], 0, 1)
    wt_gate = wt[N_A:N_A + GATE_LANES]
    wgt_hi = wt_gate.astype(BF16)
    return {
        "norm_g": norm_g[0][None, :],
        "wt_a": wt[:N_A].astype(BF16),
        "wt_b": wt[g1:].astype(BF16),
        "wgt_hi": wgt_hi,
        "wgt_lo": (wt_gate - wgt_hi.astype(F32)).astype(BF16),
        "gbias": jnp.pad(jnp.concatenate([b_ig[0], b_fg[0]]), (0, GATE_LANES - 2 * M_HEADS))[None, :],
        "conv_w": conv_w[0],
        "conv_b": conv_b[0][None, :],
        "w_qm": w_qm[0].astype(BF16),
        "w_km": w_km[0].astype(BF16),
        "mnorm_g": mnorm_g[0][None, :],
        "m_skip": m_skip[0][None, :],
        "w_brm": w_brm[0].astype(BF16),
        "hgrn_lb": hgrn_lb,
        "hnorm_g": hnorm_g[0][None, :],
        "w_brh": w_brh[0].astype(BF16),
        "w_out": w_out[0].astype(BF16),
        "w_ple": w_ple[0].astype(BF16),
        "w_pg": w_pg[0].astype(BF16),
        "final_g": final_g[None, :],
    }


def kernel(x_prompt, x_sample, state_conv, state_mlstm_C, state_mlstm_n, state_mlstm_m, state_hgrn, p_prompt, p_sample, norm_g, w_in, b_ig, b_fg, conv_w, conv_b, w_qm, w_km, mnorm_g, m_skip, w_brm, hgrn_lb, hnorm_g, w_brh, w_out, w_ple, w_pg, final_g):
    w = _prep_weights(norm_g, w_in, b_ig, b_fg, conv_w, conv_b, w_qm, w_km, mnorm_g, m_skip, w_brm, hgrn_lb,
                      hnorm_g, w_brh, w_out, w_ple, w_pg, final_g)
    out_p = _trunk(x_prompt, p_prompt[0], None, w)
    state = (state_conv[0], state_mlstm_C[0], state_mlstm_n[0], state_mlstm_m[0], state_hgrn[0])
    out_s = _trunk(x_sample, p_sample[0], state, w)
    return (out_p[0], out_s[0]) + tuple(out_p[1:]) + tuple(out_s[1:])
```

```python
import functools
import math

import jax
import jax.numpy as jnp
from jax import lax
from jax.experimental import pallas as pl
from jax.experimental.pallas import tpu as pltpu

F32 = jnp.float32
BF16 = jnp.bfloat16

D_MODEL = 1024
PLE_DIM = 256
M_HEADS = 4
M_INNER = 2 * D_MODEL
M_HD = M_INNER // M_HEADS
CONV_W = 4
H_EXPAND = 128
H_WIDTH = D_MODEL
H_HEADS = H_WIDTH // H_EXPAND
EPS = 1e-6

N_A = 3 * M_INNER
N_B = 4 * H_WIDTH + 2 * D_MODEL
GATE_LANES = 128
SUBLANES = 8
HGRN_BAND = 4
VMEM_LIMIT = 56 * 1024 * 1024
PROMPT_CHUNK = 256
PROJ_TILE = 1024
MERGE_TILE = 512
MLSTM_STREAMS = 2
HGRN_STREAMS = 4


def _dot(a, b):
    return jnp.dot(a, b, preferred_element_type=F32)


def _dot_nt(a, b):
    return lax.dot_general(a, b, (((1,), (1,)), ((), ())), preferred_element_type=F32)


def _dot_tn(a, b):
    return lax.dot_general(a, b, (((0,), (0,)), ((), ())), preferred_element_type=F32)


def _split3(x):
    hi = x.astype(BF16)
    r = x - hi.astype(F32)
    mid = r.astype(BF16)
    lo = (r - mid.astype(F32)).astype(BF16)
    return hi, mid, lo


def _sigmoid(x):
    return 1.0 / (1.0 + jnp.exp(-x))


def _silu(x):
    return x * _sigmoid(x)


def _log_sigmoid(x):
    return jnp.minimum(x, 0.0) - jnp.log(1.0 + jnp.exp(-jnp.abs(x)))


def _tri(n, lower):
    r = lax.broadcasted_iota(jnp.int32, (n, n), 0)
    c = lax.broadcasted_iota(jnp.int32, (n, n), 1)
    return jnp.where((r >= c) if lower else (r <= c), 1.0, 0.0).astype(BF16)


def _rmsnorm_split(x_ref, g_ref):
    x = x_ref[...]
    xn = x * lax.rsqrt(jnp.mean(x * x, axis=-1, keepdims=True) + EPS) * g_ref[...]
    hi = xn.astype(BF16)
    lo = (xn - hi.astype(F32)).astype(BF16)
    return hi, lo


def _gate_dot(hi, lo, wgh_ref, wgl_ref):
    return _dot_nt(hi, wgh_ref[...]) + _dot_nt(lo, wgh_ref[...]) + _dot_nt(hi, wgl_ref[...])


def _zero_after(x):
    rows, lanes = x.shape
    m = jnp.max(x.reshape(rows // SUBLANES, SUBLANES, lanes), axis=0)
    u = pltpu.bitcast(m, jnp.uint32)
    u = lax.shift_right_logical(lax.shift_right_logical(u, jnp.uint32(16)), jnp.uint32(16))
    z = pltpu.bitcast(u, F32)
    return jnp.max(z, axis=0, keepdims=True)


def _resident(shape):
    return pl.BlockSpec(shape, lambda *_: (0,) * len(shape), pipeline_mode=pl.Buffered(1))


def _proj_kernel(x_ref, g_ref, wa_ref, wb_ref, wgh_ref, wgl_ref, o_ref, gate_ref, xh_sc, *, n_a):
    j = pl.program_id(1)

    @pl.when(j == 0)
    def _():
        hi, lo = _rmsnorm_split(x_ref, g_ref)
        xh_sc[...] = hi
        gate_ref[...] = _gate_dot(hi, lo, wgh_ref, wgl_ref)

    @pl.when(j < n_a)
    def _():
        o_ref[...] = _dot_nt(xh_sc[...], wa_ref[...]).astype(BF16)

    @pl.when(j >= n_a)
    def _():
        o_ref[...] = _dot_nt(xh_sc[...], wb_ref[...]).astype(BF16)


def _proj(x2, w, tm, tn):
    n_tok = x2.shape[0]
    n_a = N_A // tn
    return pl.pallas_call(
        functools.partial(_proj_kernel, n_a=n_a),
        grid=(n_tok // tm, (N_A + N_B) // tn),
        in_specs=[
            pl.BlockSpec((tm, D_MODEL), lambda i, j: (i, 0)),
            pl.BlockSpec((1, D_MODEL), lambda i, j: (0, 0)),
            pl.BlockSpec((tn, D_MODEL), lambda i, j: (jnp.minimum(j, n_a - 1), 0)),
            pl.BlockSpec((tn, D_MODEL), lambda i, j: (jnp.maximum(j - n_a, 0), 0)),
            pl.BlockSpec((GATE_LANES, D_MODEL), lambda i, j: (N_A // GATE_LANES, 0)),
            pl.BlockSpec((GATE_LANES, D_MODEL), lambda i, j: (0, 0)),
        ],
        out_specs=[
            pl.BlockSpec((tm, tn), lambda i, j: (i, j)),
            pl.BlockSpec((tm, GATE_LANES), lambda i, j: (i, 0)),
        ],
        out_shape=[
            jax.ShapeDtypeStruct((n_tok, N_A + N_B), BF16),
            jax.ShapeDtypeStruct((n_tok, GATE_LANES), F32),
        ],
        scratch_shapes=[pltpu.VMEM((tm, D_MODEL), BF16)],
        compiler_params=pltpu.CompilerParams(
            dimension_semantics=("arbitrary", "arbitrary"), vmem_limit_bytes=VMEM_LIMIT),
        name="proj",
    )(x2, w["norm_g"], w["wt_all"], w["wt_b"], w["wt_all"], w["wgt_lo"])


def _mlstm_chunk(uext_sc, gates, v_of, z_of, Cin_ref, nin_ref, min_ref, convw_ref, convb_ref, wq_ref, wk_ref,
                 mng_ref, skip_ref, hm_ref, convo_ref, C_ref, n_ref, m_ref, L):
    TAIL = CONV_W - 1
    BASE = SUBLANES

    ext = uext_sc[...]
    conv = convb_ref[...] + ext[BASE:, :] * convw_ref[TAIL:CONV_W, :]
    for j in range(TAIL):
        conv = conv + pltpu.roll(ext, TAIL - j, axis=0)[BASE:, :] * convw_ref[j:j + 1, :]
    tail = ext[BASE + L - TAIL:, :]
    uext_sc[BASE - TAIL:BASE, :] = tail
    convo_ref[...] = tail
    c_act = _silu(conv)

    lf_cols = _log_sigmoid(gates)
    gates_t = gates.T[0:2 * M_HEADS, :]
    lf_rows = _log_sigmoid(gates_t)
    tril, triu = _tri(L, True), _tri(L, False)
    b_cols = sum(_dot(tril, part) for part in _split3(lf_cols))
    b_rows = sum(_dot(part, triu) for part in _split3(lf_rows))
    row_i = lax.broadcasted_iota(jnp.int32, (L, L), 0)
    col_i = lax.broadcasted_iota(jnp.int32, (L, L), 1)
    causal = row_i >= col_i

    done = []
    done_c = []
    for h in range(M_HEADS):
        hs = slice(h * M_HD, (h + 1) * M_HD)
        c_h = c_act[:, hs]
        ch = (c_h + done_c[h - 1] if h >= 1 else c_h).astype(BF16)
        q = _dot(ch, wq_ref[h])
        k = _dot(ch, wk_ref[h]) * (M_HD ** -0.5)
        v = v_of(h)
        q_bf = q.astype(BF16)
        k_bf = k.astype(BF16)

        ig_col = gates[:, h:h + 1]
        ig_row = gates_t[h:h + 1, :]
        b_col = b_cols[:, M_HEADS + h:M_HEADS + h + 1]
        b_row = b_rows[M_HEADS + h:M_HEADS + h + 1, :]
        m_c = min_ref[h:h + 1, 0:1]

        logd = jnp.where(causal, b_col + (ig_row - b_row), -jnp.inf)
        m_prev = b_col + m_c
        m_t = jnp.maximum(m_prev, jnp.max(logd, axis=1, keepdims=True))
        dmat = jnp.exp(logd - m_t)
        w_inter = jnp.exp(m_prev - m_t)

        C = Cin_ref[h]
        n = nin_ref[h]
        s_mat = _dot_nt(q_bf, k_bf) * dmat
        if h >= 1:
            s_mat = s_mat + done[h - 1][:, 0:L]
        num = w_inter * _dot_nt(q_bf, C.astype(BF16)) + _dot(s_mat.astype(BF16), v.astype(BF16))
        den = w_inter * jnp.sum(q * n, axis=1, keepdims=True) + jnp.sum(s_mat, axis=1, keepdims=True)
        hh = num * (1.0 / jnp.maximum(jnp.abs(den), jnp.exp(-m_t)))

        m_last = m_t[L - 1:L, :]
        w_last = jnp.exp(b_col[L - 1:L, :] - b_col + ig_col - m_last)
        decay = jnp.exp(m_prev[L - 1:L, :] - m_last)
        wv = (w_last * v).astype(BF16)
        c_new = decay * C + _dot_tn(wv, k_bf)
        C_ref[h] = c_new
        done_c.append(_zero_after(c_new))
        n_ref[h] = decay * n + jnp.sum(w_last * k, axis=0, keepdims=True)
        m_ref[h:h + 1, :] = jnp.broadcast_to(m_last, (1, m_ref.shape[1]))

        mu = jnp.mean(hh, axis=1, keepdims=True)
        d = hh - mu
        y = d * lax.rsqrt(jnp.mean(d * d, axis=1, keepdims=True) + EPS) * mng_ref[:, hs]
        out = (y + skip_ref[:, hs] * c_h) * _silu(z_of(h))
        hm_ref[:, hs] = out.astype(BF16)
        done.append(_zero_after(out))


def _mlstm_state_kernel(pm_ref, gate_ref, gbias_ref, convw_ref, convb_ref, wq_ref, wk_ref, mng_ref, skip_ref,
                        conv0_ref, C0_ref, n0_ref, m0_ref, hm_ref, convo_ref, C_ref, n_ref, m_ref, uext_sc,
                        *, L, G):
    for g in range(G):
        rows = pl.ds(g * L, L)
        ue = uext_sc.at[g]
        ue[SUBLANES - (CONV_W - 1):SUBLANES, :] = conv0_ref[g]
        ue[SUBLANES:SUBLANES + L, :] = pm_ref[rows, 0:M_INNER].astype(F32)
        v_of = lambda h, rows=rows: pm_ref[rows, M_INNER + h * M_HD:M_INNER + (h + 1) * M_HD].astype(F32)
        z_of = lambda h, rows=rows: pm_ref[rows, 2 * M_INNER + h * M_HD:2 * M_INNER + (h + 1) * M_HD].astype(F32)
        _mlstm_chunk(ue, gate_ref[rows, :] + gbias_ref[...], v_of, z_of, C0_ref.at[g], n0_ref.at[g], m0_ref.at[g],
                     convw_ref, convb_ref, wq_ref, wk_ref, mng_ref, skip_ref, hm_ref.at[rows], convo_ref.at[g],
                     C_ref.at[g], n_ref.at[g], m_ref.at[g], L)
        m_ref[g, M_HEADS:, :] = m0_ref[g, M_HEADS:, :]


def _mlstm_kernel(x_ref, ng_ref, wa_ref, wgh_ref, wgl_ref, gbias_ref, convw_ref, convb_ref,
                  wq_ref, wk_ref, mng_ref, skip_ref, hm_ref, convo_ref, C_ref, n_ref, m_ref, uext_sc, *, L):
    @pl.when(pl.program_id(1) == 0)
    def _():
        uext_sc[SUBLANES - (CONV_W - 1):SUBLANES, :] = jnp.zeros((CONV_W - 1, M_INNER), F32)
        C_ref[...] = jnp.zeros(C_ref.shape, F32)
        n_ref[...] = jnp.zeros(n_ref.shape, F32)
        m_ref[...] = jnp.zeros(m_ref.shape, F32)

    xh, xl = _rmsnorm_split(x_ref, ng_ref)
    uext_sc[SUBLANES:SUBLANES + L, :] = _dot_nt(xh, wa_ref[0:M_INNER, :])
    gates = _gate_dot(xh, xl, wgh_ref, wgl_ref) + gbias_ref[...]
    v_of = lambda h: _dot_nt(xh, wa_ref[M_INNER + h * M_HD:M_INNER + (h + 1) * M_HD, :])
    z_of = lambda h: _dot_nt(xh, wa_ref[2 * M_INNER + h * M_HD:2 * M_INNER + (h + 1) * M_HD, :])
    _mlstm_chunk(uext_sc, gates, v_of, z_of, C_ref, n_ref, m_ref, convw_ref, convb_ref,
                 wq_ref, wk_ref, mng_ref, skip_ref, hm_ref, convo_ref, C_ref, n_ref, m_ref, L)


def _state_specs(index, streams=None):
    return [
        pl.BlockSpec((streams, CONV_W - 1, M_INNER), lambda *g: (index(*g), 0, 0)),
        pl.BlockSpec((streams, M_HEADS, M_HD, M_HD), lambda *g: (index(*g), 0, 0, 0)),
        pl.BlockSpec((streams, M_HEADS, 1, M_HD), lambda *g: (index(*g), 0, 0, 0)),
        pl.BlockSpec((streams, SUBLANES, GATE_LANES), lambda *g: (index(*g), 0, 0)),
    ]


def _mlstm_out_shape(B, T):
    return [
        jax.ShapeDtypeStruct((B * T, M_INNER), BF16),
        jax.ShapeDtypeStruct((B, CONV_W - 1, M_INNER), F32),
        jax.ShapeDtypeStruct((B, M_HEADS, M_HD, M_HD), F32),
        jax.ShapeDtypeStruct((B, M_HEADS, 1, M_HD), F32),
        jax.ShapeDtypeStruct((B, SUBLANES, GATE_LANES), F32),
    ]


def _mlstm_shared(w):
    return [w["gbias"], w["conv_w"], w["conv_b"], w["w_qm"], w["w_km"], w["mnorm_g"], w["m_skip"]]


def _mlstm_state(proj, gates, w, state, B, L, G):
    shared = _mlstm_shared(w)
    tok = lambda b: (b, 0)
    state_specs = _state_specs(lambda b: b, streams=G)
    return pl.pallas_call(
        functools.partial(_mlstm_state_kernel, L=L, G=G),
        grid=(B // G,),
        in_specs=[pl.BlockSpec((G * L, N_A), tok), pl.BlockSpec((G * L, GATE_LANES), tok)]
        + [_resident(a.shape) for a in shared] + state_specs,
        out_specs=[pl.BlockSpec((G * L, M_INNER), tok)] + state_specs,
        out_shape=_mlstm_out_shape(B, L),
        scratch_shapes=[pltpu.VMEM((G, SUBLANES + L, M_INNER), F32)],
        compiler_params=pltpu.CompilerParams(dimension_semantics=("arbitrary",), vmem_limit_bytes=VMEM_LIMIT),
        name="mlstm_state",
    )(proj, gates, *shared, *state)


def _mlstm(x2, w, B, T, L):
    NC = T // L
    tok = lambda b, c: (b * NC + c, 0)
    consts = [w["norm_g"], w["wt_all"], w["wt_all"], w["wgt_lo"]] + _mlstm_shared(w)
    front_specs = [
        _resident(w["norm_g"].shape),
        pl.BlockSpec((N_A, D_MODEL), lambda b, c: (0, 0), pipeline_mode=pl.Buffered(1)),
        pl.BlockSpec((GATE_LANES, D_MODEL), lambda b, c: (N_A // GATE_LANES, 0), pipeline_mode=pl.Buffered(1)),
    ]
    return pl.pallas_call(
        functools.partial(_mlstm_kernel, L=L),
        grid=(B, NC),
        in_specs=[pl.BlockSpec((L, D_MODEL), tok)] + front_specs + [_resident(a.shape) for a in consts[3:]],
        out_specs=[pl.BlockSpec((L, M_INNER), tok)] + _state_specs(lambda b, c: b),
        out_shape=_mlstm_out_shape(B, T),
        scratch_shapes=[pltpu.VMEM((SUBLANES + L, M_INNER), F32)],
        compiler_params=pltpu.CompilerParams(
            dimension_semantics=("arbitrary", "arbitrary"), vmem_limit_bytes=VMEM_LIMIT),
        name="mlstm",
    )(x2, *consts)


def _hgrn_chunk(q_pre, f, v, gz, lb_ref, hng_ref, st_sc, oh_ref, L):
    HD = H_EXPAND
    W = H_WIDTH
    v_bf = v.astype(BF16)

    lbp = lb_ref[...]
    lbm = jnp.max(lbp, axis=0, keepdims=True)
    lbe = jnp.exp(lbp - lbm)
    lb = lbe[0:1, :] / jnp.sum(lbe, axis=0, keepdims=True)

    e = jnp.exp(-jnp.abs(f))
    r = 1.0 / (1.0 + e)
    sig_pos = jnp.where(f >= 0, r, e * r)
    sig_neg = jnp.where(f >= 0, e * r, r)
    fgt = lb + (1.0 - lb) * sig_pos
    lf = jnp.log(fgt)
    kk = (1.0 - lb) * sig_neg
    qa = _silu(q_pre)

    tril = _tri(L, True)
    g = sum(_dot(tril, part) for part in _split3(lf))
    g_last = g[L - 1:L, :]
    qa_bf = qa.astype(BF16)
    kk_bf = kk.astype(BF16)
    q_in = qa_bf * jnp.exp(g).astype(BF16)
    k_out = kk_bf * jnp.exp(g_last - g).astype(BF16)
    s_dec = jnp.exp(g_last)

    row_i = lax.broadcasted_iota(jnp.int32, (L, L), 0)
    col_i = lax.broadcasted_iota(jnp.int32, (L, L), 1)
    levels = []
    hb = HGRN_BAND
    while hb < L:
        nb = L // (2 * hb)
        g3 = g.reshape(nb, 2 * hb, W)
        ge = jnp.broadcast_to(g3[:, hb - 1:hb, :], (nb, 2 * hb, W)).reshape(L, W)
        e_l = jnp.exp(-jnp.abs(g - ge)).astype(BF16)
        q_l = qa_bf * e_l
        k_l = kk_bf * e_l
        sh = (2 * hb).bit_length() - 1
        same_blk = jnp.right_shift(row_i, sh) == jnp.right_shift(col_i, sh)
        mask = same_blk & (jnp.bitwise_and(row_i, hb) != 0) & (jnp.bitwise_and(col_i, hb) == 0)
        levels.append((q_l, k_l, mask))
        hb *= 2

    nblk = L // SUBLANES
    to_blocks = lambda a: a.reshape(nblk, SUBLANES, W)
    sub_i = lax.broadcasted_iota(jnp.int32, (nblk, SUBLANES, W), 1)
    f0 = jnp.where(jnp.bitwise_and(sub_i, HGRN_BAND - 1) == 0, 0.0, to_blocks(fgt))
    qa3 = to_blocks(qa)
    ke = to_blocks(kk)
    vs = to_blocks(v)
    o_band = [None] * H_HEADS
    for d in range(HGRN_BAND):
        if d > 0:
            ke = f0 * pltpu.roll(ke, 1, axis=1)
            vs = pltpu.roll(vs, 1, axis=1)
        p_d = qa3 * ke
        for h in range(H_HEADS):
            hs = slice(h * HD, (h + 1) * HD)
            term = jnp.sum(p_d[:, :, hs], axis=2, keepdims=True) * vs[:, :, hs]
            o_band[h] = term if d == 0 else o_band[h] + term

    for h in range(H_HEADS):
        hs = slice(h * HD, (h + 1) * HD)
        st = st_sc[h]
        o = _dot_nt(q_in[:, hs], st.astype(BF16))
        a = jnp.zeros((L, L), F32)
        for q_l, k_l, mask in levels:
            a = jnp.where(mask, _dot_nt(q_l[:, hs], k_l[:, hs]), a)
        o = o + _dot(a.astype(BF16), v_bf[:, hs]) + o_band[h].reshape(L, HD)
        st_sc[h] = st * s_dec[:, hs] + _dot_tn(v_bf[:, hs], k_out[:, hs])
        y = o * lax.rsqrt(jnp.mean(o * o, axis=1, keepdims=True) + EPS) * hng_ref[:, hs]
        oh_ref[:, hs] = (y * _silu(gz[:, hs])).astype(BF16)


def _merge_math(x, p, hm, oh, gab, wbrm_ref, wbrh_ref, wout_ref, wpg_ref, wple_ref, fg_ref):
    ya = _dot(hm, wbrm_ref[...])
    yb = _dot(oh, wbrh_ref[...])
    ga = gab[:, 0:D_MODEL].astype(F32)
    gb = gab[:, D_MODEL:2 * D_MODEL].astype(F32)
    y = _sigmoid(ga) * ya + _sigmoid(gb) * yb
    h1 = x + _dot(y.astype(BF16), wout_ref[...])
    pe = _dot(p.astype(BF16), wple_ref[...])
    h2 = h1 + _sigmoid(_dot(h1.astype(BF16), wpg_ref[...])) * pe
    return h2 * lax.rsqrt(jnp.mean(h2 * h2, axis=-1, keepdims=True) + EPS) * fg_ref[...]


def _hgrn_state_kernel(qf_ref, ig_ref, lb_ref, hng_ref, S0_ref, oh_ref, S_ref, st_sc, *, L, G):
    W = H_WIDTH
    for g in range(G):
        rows = pl.ds(g * L, L)
        st = st_sc.at[g]
        for h in range(H_HEADS):
            st[h] = S0_ref[g, h].T
        _hgrn_chunk(qf_ref[rows, 0:W].astype(F32), qf_ref[rows, W:2 * W].astype(F32),
                    ig_ref[rows, 0:W].astype(F32), ig_ref[rows, W:2 * W].astype(F32),
                    lb_ref, hng_ref, st, oh_ref.at[rows], L)
        for h in range(H_HEADS):
            S_ref[g, h] = st[h].T


def _hgrn_state(proj, w, S0, B, L, G):
    off = N_A // (2 * H_WIDTH)
    s_spec = pl.BlockSpec((G, H_HEADS, H_EXPAND, H_EXPAND), lambda b: (b, 0, 0, 0))
    return pl.pallas_call(
        functools.partial(_hgrn_state_kernel, L=L, G=G),
        grid=(B // G,),
        in_specs=[
            pl.BlockSpec((G * L, 2 * H_WIDTH), lambda b: (b, off)),
            pl.BlockSpec((G * L, 2 * H_WIDTH), lambda b: (b, off + 1)),
            _resident(w["hgrn_lb"].shape), _resident(w["hnorm_g"].shape), s_spec,
        ],
        out_specs=[pl.BlockSpec((G * L, H_WIDTH), lambda b: (b, 0)), s_spec],
        out_shape=[jax.ShapeDtypeStruct((B * L, H_WIDTH), BF16),
                   jax.ShapeDtypeStruct((B, H_HEADS, H_EXPAND, H_EXPAND), F32)],
        scratch_shapes=[pltpu.VMEM((G, H_HEADS, H_EXPAND, H_EXPAND), F32)],
        compiler_params=pltpu.CompilerParams(dimension_semantics=("arbitrary",), vmem_limit_bytes=VMEM_LIMIT),
        name="hgrn_state",
    )(proj, proj, w["hgrn_lb"], w["hnorm_g"], S0)


def _hgrn_kernel(x_ref, ng_ref, wb_ref, lb_ref, hng_ref, oh_ref, gab_ref, S_ref, st_sc, *, L):
    W = H_WIDTH

    @pl.when(pl.program_id(1) == 0)
    def _():
        st_sc[...] = jnp.zeros(st_sc.shape, F32)

    xh, _ = _rmsnorm_split(x_ref, ng_ref)
    q_pre = _dot_nt(xh, wb_ref[0:W, :])
    f = _dot_nt(xh, wb_ref[W:2 * W, :])
    v = _dot_nt(xh, wb_ref[2 * W:3 * W, :])
    gz = _dot_nt(xh, wb_ref[3 * W:4 * W, :])
    gab_ref[...] = _dot_nt(xh, wb_ref[4 * W:N_B, :]).astype(BF16)
    _hgrn_chunk(q_pre, f, v, gz, lb_ref, hng_ref, st_sc, oh_ref, L)

    @pl.when(pl.program_id(1) == pl.num_programs(1) - 1)
    def _():
        for h in range(H_HEADS):
            S_ref[h] = st_sc[h].T


def _hgrn(x2, w, B, T, L):
    NC = T // L
    tok = lambda b, c: (b * NC + c, 0)
    consts = [w["norm_g"], w["wt_b"], w["hgrn_lb"], w["hnorm_g"]]
    return pl.pallas_call(
        functools.partial(_hgrn_kernel, L=L),
        grid=(B, NC),
        in_specs=[pl.BlockSpec((L, D_MODEL), tok)] + [_resident(a.shape) for a in consts],
        out_specs=[pl.BlockSpec((L, H_WIDTH), tok), pl.BlockSpec((L, 2 * D_MODEL), tok),
                   pl.BlockSpec((None, H_HEADS, H_EXPAND, H_EXPAND), lambda b, c: (b, 0, 0, 0))],
        out_shape=[jax.ShapeDtypeStruct((B * T, H_WIDTH), BF16),
                   jax.ShapeDtypeStruct((B * T, 2 * D_MODEL), BF16),
                   jax.ShapeDtypeStruct((B, H_HEADS, H_EXPAND, H_EXPAND), F32)],
        scratch_shapes=[pltpu.VMEM((H_HEADS, H_EXPAND, H_EXPAND), F32)],
        compiler_params=pltpu.CompilerParams(
            dimension_semantics=("arbitrary", "arbitrary"), vmem_limit_bytes=VMEM_LIMIT),
        name="hgrn",
    )(x2, *consts)


def _merge_kernel(x_ref, p_ref, hm_ref, oh_ref, gab_ref, wbrm_ref, wbrh_ref, wout_ref, wpg_ref, wple_ref,
                  fg_ref, o_ref):
    o_ref[...] = _merge_math(x_ref[...], p_ref[...], hm_ref[...], oh_ref[...], gab_ref[...],
                             wbrm_ref, wbrh_ref, wout_ref, wpg_ref, wple_ref, fg_ref)


def _merge(x2, p2, hm, oh, gab, gab_blk, w, tm):
    n_tok = x2.shape[0]
    tok = lambda i: (i, 0)
    consts = [w["w_brm"], w["w_brh"], w["w_out"], w["w_pg"], w["w_ple"], w["final_g"]]
    return pl.pallas_call(
        _merge_kernel,
        grid=(n_tok // tm,),
        in_specs=[
            pl.BlockSpec((tm, D_MODEL), tok),
            pl.BlockSpec((tm, PLE_DIM), tok),
            pl.BlockSpec((tm, M_INNER), tok),
            pl.BlockSpec((tm, H_WIDTH), tok),
            pl.BlockSpec((tm, 2 * D_MODEL), lambda i: (i, gab_blk)),
        ] + [_resident(a.shape) for a in consts],
        out_specs=pl.BlockSpec((tm, D_MODEL), tok),
        out_shape=jax.ShapeDtypeStruct((n_tok, D_MODEL), F32),
        compiler_params=pltpu.CompilerParams(
            dimension_semantics=("arbitrary",), vmem_limit_bytes=VMEM_LIMIT),
        name="merge",
    )(x2, p2, hm, oh, gab, *consts)


def _trunk(x, p, state, w):
    B, T, _ = x.shape
    x2 = x.reshape(B * T, D_MODEL)
    p2 = p.reshape(B * T, PLE_DIM)
    tm = min(MERGE_TILE, B * T)
    if state is None:
        L = min(PROMPT_CHUNK, T)
        hm, conv_n, C_n, n_n, m_n = _mlstm(x2, w, B, T, L)
        oh, gab, S_n = _hgrn(x2, w, B, T, L)
        y = _merge(x2, p2, hm, oh, gab, 0, w, tm)
    else:
        conv0, C0, n0, m0, S0 = state
        proj, gates = _proj(x2, w, min(PROJ_TILE, B * T), 2 * D_MODEL)
        m0p = jnp.zeros((B, SUBLANES, GATE_LANES), F32).at[:, :M_HEADS, :].set(m0[:, :, None])
        m_state = (conv0, C0, n0.reshape(B, M_HEADS, 1, M_HD), m0p)
        hm, conv_n, C_n, n_n, m_n = _mlstm_state(proj, gates, w, m_state, B, T, math.gcd(B, MLSTM_STREAMS))
        oh, S_n = _hgrn_state(proj, w, S0, B, T, math.gcd(B, HGRN_STREAMS))
        y = _merge(x2, p2, hm, oh, proj, (N_A + 4 * H_WIDTH) // (2 * D_MODEL), w, tm)
    return (y.reshape(B, T, D_MODEL), conv_n[None], C_n[None], n_n.reshape(B, M_HEADS, M_HD)[None],
            m_n[:, :M_HEADS, 0][None], S_n[None])


def _prep_weights(norm_g, w_in, b_ig, b_fg, conv_w, conv_b, w_qm, w_km, mnorm_g, m_skip, w_brm, hgrn_lb, hnorm_g,
                  w_brh, w_out, w_ple, w_pg, final_g):
    assert norm_g.shape[0] == 1, "single-layer trunk"
    g1 = N_A + 2 * M_HEADS
    wt = jnp.swapaxes(w_in[0], 0, 1)
    wt_gate = wt[N_A:N_A + GATE_LANES]
    wt_all = wt.astype(BF16)
    return {
        "norm_g": norm_g[0][None, :],
        "wt_all": wt_all,
        "wt_b": wt_all[g1:],
        "wgt_lo": (wt_gate - wt_gate.astype(BF16).astype(F32)).astype(BF16),
        "gbias": jnp.pad(jnp.concatenate([b_ig[0], b_fg[0]]), (0, GATE_LANES - 2 * M_HEADS))[None, :],
        "conv_w": conv_w[0],
        "conv_b": conv_b[0][None, :],
        "w_qm": w_qm[0].astype(BF16),
        "w_km": w_km[0].astype(BF16),
        "mnorm_g": mnorm_g[0][None, :],
        "m_skip": m_skip[0][None, :],
        "w_brm": w_brm[0].astype(BF16),
        "hgrn_lb": hgrn_lb,
        "hnorm_g": hnorm_g[0][None, :],
        "w_brh": w_brh[0].astype(BF16),
        "w_out": w_out[0].astype(BF16),
        "w_ple": w_ple[0].astype(BF16),
        "w_pg": w_pg[0].astype(BF16),
        "final_g": final_g[None, :],
    }


def kernel(x_prompt, x_sample, state_conv, state_mlstm_C, state_mlstm_n, state_mlstm_m, state_hgrn, p_prompt, p_sample, norm_g, w_in, b_ig, b_fg, conv_w, conv_b, w_qm, w_km, mnorm_g, m_skip, w_brm, hgrn_lb, hnorm_g, w_brh, w_out, w_ple, w_pg, final_g):
    w = _prep_weights(norm_g, w_in, b_ig, b_fg, conv_w, conv_b, w_qm, w_km, mnorm_g, m_skip, w_brm, hgrn_lb,
                      hnorm_g, w_brh, w_out, w_ple, w_pg, final_g)
    out_p = _trunk(x_prompt, p_prompt[0], None, w)
    state = (state_conv[0], state_mlstm_C[0], state_mlstm_n[0], state_mlstm_m[0], state_hgrn[0])
    out_s = _trunk(x_sample, p_sample[0], state, w)
    return (out_p[0], out_s[0]) + tuple(out_p[1:]) + tuple(out_s[1:])
```

```python
import functools
import math

import jax
import jax.numpy as jnp
from jax import lax
from jax.experimental import pallas as pl
from jax.experimental.pallas import tpu as pltpu

F32 = jnp.float32
BF16 = jnp.bfloat16

D_MODEL = 1024
PLE_DIM = 256
M_HEADS = 4
M_INNER = 2 * D_MODEL
M_HD = M_INNER // M_HEADS
CONV_W = 4
H_EXPAND = 128
H_WIDTH = D_MODEL
H_HEADS = H_WIDTH // H_EXPAND
EPS = 1e-6

N_A = 3 * M_INNER
N_B = 4 * H_WIDTH + 2 * D_MODEL
GATE_LANES = 128
SUBLANES = 8
HGRN_BAND = 4
VMEM_LIMIT = 56 * 1024 * 1024
PROMPT_CHUNK = 256
PROJ_TILE = 1024
MERGE_TILE = 512
MLSTM_STREAMS = 2
HGRN_STREAMS = 4


def _dot(a, b):
    return jnp.dot(a, b, preferred_element_type=F32)


def _dot_nt(a, b):
    return lax.dot_general(a, b, (((1,), (1,)), ((), ())), preferred_element_type=F32)


def _dot_tn(a, b):
    return lax.dot_general(a, b, (((0,), (0,)), ((), ())), preferred_element_type=F32)


def _split3(x):
    hi = x.astype(BF16)
    r = x - hi.astype(F32)
    mid = r.astype(BF16)
    lo = (r - mid.astype(F32)).astype(BF16)
    return hi, mid, lo


def _sigmoid(x):
    return 1.0 / (1.0 + jnp.exp(-x))


def _silu(x):
    return x * _sigmoid(x)


def _log_sigmoid(x):
    return jnp.minimum(x, 0.0) - jnp.log(1.0 + jnp.exp(-jnp.abs(x)))


def _tri(n, lower):
    r = lax.broadcasted_iota(jnp.int32, (n, n), 0)
    c = lax.broadcasted_iota(jnp.int32, (n, n), 1)
    return jnp.where((r >= c) if lower else (r <= c), 1.0, 0.0).astype(BF16)


def _rmsnorm_split(x_ref, g_ref):
    x = x_ref[...]
    xn = x * lax.rsqrt(jnp.mean(x * x, axis=-1, keepdims=True) + EPS) * g_ref[...]
    hi = xn.astype(BF16)
    lo = (xn - hi.astype(F32)).astype(BF16)
    return hi, lo


def _gate_dot(hi, lo, wgh_ref, wgl_ref):
    return _dot_nt(hi, wgh_ref[...]) + _dot_nt(lo, wgh_ref[...]) + _dot_nt(hi, wgl_ref[...])


def _zero_after(x):
    rows, lanes = x.shape
    m = jnp.max(x.reshape(rows // SUBLANES, SUBLANES, lanes), axis=0)
    u = pltpu.bitcast(m, jnp.uint32)
    u = lax.shift_right_logical(lax.shift_right_logical(u, jnp.uint32(16)), jnp.uint32(16))
    z = pltpu.bitcast(u, F32)
    return jnp.max(z, axis=0, keepdims=True)


def _resident(shape):
    return pl.BlockSpec(shape, lambda *_: (0,) * len(shape), pipeline_mode=pl.Buffered(1))


def _proj_kernel(x_ref, g_ref, wa_ref, wb_ref, wgh_ref, wgl_ref, o_ref, gate_ref, xh_sc, *, n_a):
    j = pl.program_id(1)

    @pl.when(j == 0)
    def _():
        hi, lo = _rmsnorm_split(x_ref, g_ref)
        xh_sc[...] = hi
        gate_ref[...] = _gate_dot(hi, lo, wgh_ref, wgl_ref)

    @pl.when(j < n_a)
    def _():
        o_ref[...] = _dot_nt(xh_sc[...], wa_ref[...]).astype(BF16)

    @pl.when(j >= n_a)
    def _():
        o_ref[...] = _dot_nt(xh_sc[...], wb_ref[...]).astype(BF16)


def _proj(x2, w, tm, tn):
    n_tok = x2.shape[0]
    n_a = N_A // tn
    return pl.pallas_call(
        functools.partial(_proj_kernel, n_a=n_a),
        grid=(n_tok // tm, (N_A + N_B) // tn),
        in_specs=[
            pl.BlockSpec((tm, D_MODEL), lambda i, j: (i, 0)),
            pl.BlockSpec((1, D_MODEL), lambda i, j: (0, 0)),
            pl.BlockSpec((tn, D_MODEL), lambda i, j: (jnp.minimum(j, n_a - 1), 0)),
            pl.BlockSpec((tn, D_MODEL), lambda i, j: (jnp.maximum(j - n_a, 0), 0)),
            pl.BlockSpec((GATE_LANES, D_MODEL), lambda i, j: (N_A // GATE_LANES, 0)),
            pl.BlockSpec((GATE_LANES, D_MODEL), lambda i, j: (0, 0)),
        ],
        out_specs=[
            pl.BlockSpec((tm, tn), lambda i, j: (i, j)),
            pl.BlockSpec((tm, GATE_LANES), lambda i, j: (i, 0)),
        ],
        out_shape=[
            jax.ShapeDtypeStruct((n_tok, N_A + N_B), BF16),
            jax.ShapeDtypeStruct((n_tok, GATE_LANES), F32),
        ],
        scratch_shapes=[pltpu.VMEM((tm, D_MODEL), BF16)],
        compiler_params=pltpu.CompilerParams(
            dimension_semantics=("arbitrary", "arbitrary"), vmem_limit_bytes=VMEM_LIMIT),
        name="proj",
    )(x2, w["norm_g"], w["wt_all"], w["wt_b"], w["wt_all"], w["wgt_lo"])


def _mlstm_chunk(uext_sc, gates, v_of, z_of, Cin_ref, nin_ref, min_ref, convw_ref, convb_ref, wq_ref, wk_ref,
                 mng_ref, skip_ref, hm_ref, convo_ref, C_ref, n_ref, m_ref, L):
    TAIL = CONV_W - 1
    BASE = SUBLANES

    ext = uext_sc[...]
    conv = convb_ref[...] + ext[BASE:, :] * convw_ref[TAIL:CONV_W, :]
    for j in range(TAIL):
        conv = conv + pltpu.roll(ext, TAIL - j, axis=0)[BASE:, :] * convw_ref[j:j + 1, :]
    tail = ext[BASE + L - TAIL:, :]
    uext_sc[BASE - TAIL:BASE, :] = tail
    convo_ref[...] = tail
    c_act = _silu(conv)

    lf_cols = _log_sigmoid(gates)
    gates_t = gates.T[0:2 * M_HEADS, :]
    lf_rows = _log_sigmoid(gates_t)
    tril, triu = _tri(L, True), _tri(L, False)
    b_cols = sum(_dot(tril, part) for part in _split3(lf_cols))
    b_rows = sum(_dot(part, triu) for part in _split3(lf_rows))
    row_i = lax.broadcasted_iota(jnp.int32, (L, L), 0)
    col_i = lax.broadcasted_iota(jnp.int32, (L, L), 1)
    causal = row_i >= col_i

    done = []
    done_c = []
    for h in range(M_HEADS):
        hs = slice(h * M_HD, (h + 1) * M_HD)
        c_h = c_act[:, hs]
        ch = (c_h + done_c[h - 1] if h >= 1 else c_h).astype(BF16)
        q = _dot(ch, wq_ref[h])
        k = _dot(ch, wk_ref[h]) * (M_HD ** -0.5)
        v = v_of(h)
        q_bf = q.astype(BF16)
        k_bf = k.astype(BF16)

        ig_col = gates[:, h:h + 1]
        ig_row = gates_t[h:h + 1, :]
        b_col = b_cols[:, M_HEADS + h:M_HEADS + h + 1]
        b_row = b_rows[M_HEADS + h:M_HEADS + h + 1, :]
        m_c = min_ref[h:h + 1, 0:1]

        logd = jnp.where(causal, b_col + (ig_row - b_row), -jnp.inf)
        m_prev = b_col + m_c
        m_t = jnp.maximum(m_prev, jnp.max(logd, axis=1, keepdims=True))
        dmat = jnp.exp(logd - m_t)
        w_inter = jnp.exp(m_prev - m_t)

        C = Cin_ref[h]
        n = nin_ref[h]
        s_mat = _dot_nt(q_bf, k_bf) * dmat
        if h >= 1:
            s_mat = s_mat + done[h - 1][:, 0:L]
        num = w_inter * _dot_nt(q_bf, C.astype(BF16)) + _dot(s_mat.astype(BF16), v.astype(BF16))
        den = w_inter * jnp.sum(q * n, axis=1, keepdims=True) + jnp.sum(s_mat, axis=1, keepdims=True)
        hh = num * (1.0 / jnp.maximum(jnp.abs(den), jnp.exp(-m_t)))

        m_last = m_t[L - 1:L, :]
        w_last = jnp.exp(b_col[L - 1:L, :] - b_col + ig_col - m_last)
        decay = jnp.exp(m_prev[L - 1:L, :] - m_last)
        wv = (w_last * v).astype(BF16)
        c_new = decay * C + _dot_tn(wv, k_bf)
        C_ref[h] = c_new
        done_c.append(_zero_after(c_new[M_HD - M_HD // 4:, :]))
        n_ref[h] = decay * n + jnp.sum(w_last * k, axis=0, keepdims=True)
        m_ref[h:h + 1, :] = jnp.broadcast_to(m_last, (1, m_ref.shape[1]))

        mu = jnp.mean(hh, axis=1, keepdims=True)
        d = hh - mu
        y = d * lax.rsqrt(jnp.mean(d * d, axis=1, keepdims=True) + EPS) * mng_ref[:, hs]
        out = (y + skip_ref[:, hs] * c_h) * _silu(z_of(h))
        hm_ref[:, hs] = out.astype(BF16)
        done.append(_zero_after(out))


def _mlstm_state_kernel(pm_ref, gate_ref, gbias_ref, convw_ref, convb_ref, wq_ref, wk_ref, mng_ref, skip_ref,
                        conv0_ref, C0_ref, n0_ref, m0_ref, hm_ref, convo_ref, C_ref, n_ref, m_ref, uext_sc,
                        *, L, G):
    for g in range(G):
        rows = pl.ds(g * L, L)
        ue = uext_sc.at[g]
        ue[SUBLANES - (CONV_W - 1):SUBLANES, :] = conv0_ref[g]
        ue[SUBLANES:SUBLANES + L, :] = pm_ref[rows, 0:M_INNER].astype(F32)
        v_of = lambda h, rows=rows: pm_ref[rows, M_INNER + h * M_HD:M_INNER + (h + 1) * M_HD].astype(F32)
        z_of = lambda h, rows=rows: pm_ref[rows, 2 * M_INNER + h * M_HD:2 * M_INNER + (h + 1) * M_HD].astype(F32)
        _mlstm_chunk(ue, gate_ref[rows, :] + gbias_ref[...], v_of, z_of, C0_ref.at[g], n0_ref.at[g], m0_ref.at[g],
                     convw_ref, convb_ref, wq_ref, wk_ref, mng_ref, skip_ref, hm_ref.at[rows], convo_ref.at[g],
                     C_ref.at[g], n_ref.at[g], m_ref.at[g], L)
        m_ref[g, M_HEADS:, :] = m0_ref[g, M_HEADS:, :]


def _mlstm_kernel(x_ref, ng_ref, wa_ref, wgh_ref, wgl_ref, gbias_ref, convw_ref, convb_ref,
                  wq_ref, wk_ref, mng_ref, skip_ref, hm_ref, convo_ref, C_ref, n_ref, m_ref, uext_sc, *, L):
    @pl.when(pl.program_id(1) == 0)
    def _():
        uext_sc[SUBLANES - (CONV_W - 1):SUBLANES, :] = jnp.zeros((CONV_W - 1, M_INNER), F32)
        C_ref[...] = jnp.zeros(C_ref.shape, F32)
        n_ref[...] = jnp.zeros(n_ref.shape, F32)
        m_ref[...] = jnp.zeros(m_ref.shape, F32)

    xh, xl = _rmsnorm_split(x_ref, ng_ref)
    uext_sc[SUBLANES:SUBLANES + L, :] = _dot_nt(xh, wa_ref[0:M_INNER, :])
    gates = _gate_dot(xh, xl, wgh_ref, wgl_ref) + gbias_ref[...]
    v_of = lambda h: _dot_nt(xh, wa_ref[M_INNER + h * M_HD:M_INNER + (h + 1) * M_HD, :])
    z_of = lambda h: _dot_nt(xh, wa_ref[2 * M_INNER + h * M_HD:2 * M_INNER + (h + 1) * M_HD, :])
    _mlstm_chunk(uext_sc, gates, v_of, z_of, C_ref, n_ref, m_ref, convw_ref, convb_ref,
                 wq_ref, wk_ref, mng_ref, skip_ref, hm_ref, convo_ref, C_ref, n_ref, m_ref, L)


def _state_specs(index, streams=None):
    return [
        pl.BlockSpec((streams, CONV_W - 1, M_INNER), lambda *g: (index(*g), 0, 0)),
        pl.BlockSpec((streams, M_HEADS, M_HD, M_HD), lambda *g: (index(*g), 0, 0, 0)),
        pl.BlockSpec((streams, M_HEADS, 1, M_HD), lambda *g: (index(*g), 0, 0, 0)),
        pl.BlockSpec((streams, SUBLANES, GATE_LANES), lambda *g: (index(*g), 0, 0)),
    ]


def _mlstm_out_shape(B, T):
    return [
        jax.ShapeDtypeStruct((B * T, M_INNER), BF16),
        jax.ShapeDtypeStruct((B, CONV_W - 1, M_INNER), F32),
        jax.ShapeDtypeStruct((B, M_HEADS, M_HD, M_HD), F32),
        jax.ShapeDtypeStruct((B, M_HEADS, 1, M_HD), F32),
        jax.ShapeDtypeStruct((B, SUBLANES, GATE_LANES), F32),
    ]


def _mlstm_shared(w):
    return [w["gbias"], w["conv_w"], w["conv_b"], w["w_qm"], w["w_km"], w["mnorm_g"], w["m_skip"]]


def _mlstm_state(proj, gates, w, state, B, L, G):
    shared = _mlstm_shared(w)
    tok = lambda b: (b, 0)
    state_specs = _state_specs(lambda b: b, streams=G)
    return pl.pallas_call(
        functools.partial(_mlstm_state_kernel, L=L, G=G),
        grid=(B // G,),
        in_specs=[pl.BlockSpec((G * L, N_A), tok), pl.BlockSpec((G * L, GATE_LANES), tok)]
        + [_resident(a.shape) for a in shared] + state_specs,
        out_specs=[pl.BlockSpec((G * L, M_INNER), tok)] + state_specs,
        out_shape=_mlstm_out_shape(B, L),
        scratch_shapes=[pltpu.VMEM((G, SUBLANES + L, M_INNER), F32)],
        compiler_params=pltpu.CompilerParams(dimension_semantics=("arbitrary",), vmem_limit_bytes=VMEM_LIMIT),
        name="mlstm_state",
    )(proj, gates, *shared, *state)


def _mlstm(x2, w, B, T, L):
    NC = T // L
    tok = lambda b, c: (b * NC + c, 0)
    consts = [w["norm_g"], w["wt_all"], w["wt_all"], w["wgt_lo"]] + _mlstm_shared(w)
    front_specs = [
        _resident(w["norm_g"].shape),
        pl.BlockSpec((N_A, D_MODEL), lambda b, c: (0, 0), pipeline_mode=pl.Buffered(1)),
        pl.BlockSpec((GATE_LANES, D_MODEL), lambda b, c: (N_A // GATE_LANES, 0), pipeline_mode=pl.Buffered(1)),
    ]
    return pl.pallas_call(
        functools.partial(_mlstm_kernel, L=L),
        grid=(B, NC),
        in_specs=[pl.BlockSpec((L, D_MODEL), tok)] + front_specs + [_resident(a.shape) for a in consts[3:]],
        out_specs=[pl.BlockSpec((L, M_INNER), tok)] + _state_specs(lambda b, c: b),
        out_shape=_mlstm_out_shape(B, T),
        scratch_shapes=[pltpu.VMEM((SUBLANES + L, M_INNER), F32)],
        compiler_params=pltpu.CompilerParams(
            dimension_semantics=("arbitrary", "arbitrary"), vmem_limit_bytes=VMEM_LIMIT),
        name="mlstm",
    )(x2, *consts)


def _hgrn_chunk(q_pre, f, v, gz, lb_ref, hng_ref, st_sc, oh_ref, L):
    HD = H_EXPAND
    W = H_WIDTH
    v_bf = v.astype(BF16)

    lbp = lb_ref[...]
    lbm = jnp.max(lbp, axis=0, keepdims=True)
    lbe = jnp.exp(lbp - lbm)
    lb = lbe[0:1, :] / jnp.sum(lbe, axis=0, keepdims=True)

    e = jnp.exp(-jnp.abs(f))
    r = 1.0 / (1.0 + e)
    sig_pos = jnp.where(f >= 0, r, e * r)
    sig_neg = jnp.where(f >= 0, e * r, r)
    fgt = lb + (1.0 - lb) * sig_pos
    lf = jnp.log(fgt)
    kk = (1.0 - lb) * sig_neg
    qa = _silu(q_pre)

    tril = _tri(L, True)
    g = sum(_dot(tril, part) for part in _split3(lf))
    g_last = g[L - 1:L, :]
    qa_bf = qa.astype(BF16)
    kk_bf = kk.astype(BF16)
    q_in = qa_bf * jnp.exp(g).astype(BF16)
    k_out = kk_bf * jnp.exp(g_last - g).astype(BF16)
    s_dec = jnp.exp(g_last)

    row_i = lax.broadcasted_iota(jnp.int32, (L, L), 0)
    col_i = lax.broadcasted_iota(jnp.int32, (L, L), 1)
    levels = []
    hb = HGRN_BAND
    while hb < L:
        nb = L // (2 * hb)
        g3 = g.reshape(nb, 2 * hb, W)
        ge = jnp.broadcast_to(g3[:, hb - 1:hb, :], (nb, 2 * hb, W)).reshape(L, W)
        e_l = jnp.exp(-jnp.abs(g - ge)).astype(BF16)
        q_l = qa_bf * e_l
        k_l = kk_bf * e_l
        sh = (2 * hb).bit_length() - 1
        same_blk = jnp.right_shift(row_i, sh) == jnp.right_shift(col_i, sh)
        mask = same_blk & (jnp.bitwise_and(row_i, hb) != 0) & (jnp.bitwise_and(col_i, hb) == 0)
        levels.append((q_l, k_l, mask))
        hb *= 2

    nblk = L // SUBLANES
    to_blocks = lambda a: a.reshape(nblk, SUBLANES, W)
    sub_i = lax.broadcasted_iota(jnp.int32, (nblk, SUBLANES, W), 1)
    f0 = jnp.where(jnp.bitwise_and(sub_i, HGRN_BAND - 1) == 0, 0.0, to_blocks(fgt))
    qa3 = to_blocks(qa)
    ke = to_blocks(kk)
    vs = to_blocks(v)
    o_band = [None] * H_HEADS
    for d in range(HGRN_BAND):
        if d > 0:
            ke = f0 * pltpu.roll(ke, 1, axis=1)
            vs = pltpu.roll(vs, 1, axis=1)
        p_d = qa3 * ke
        for h in range(H_HEADS):
            hs = slice(h * HD, (h + 1) * HD)
            term = jnp.sum(p_d[:, :, hs], axis=2, keepdims=True) * vs[:, :, hs]
            o_band[h] = term if d == 0 else o_band[h] + term

    for h in range(H_HEADS):
        hs = slice(h * HD, (h + 1) * HD)
        st = st_sc[h]
        o = _dot_nt(q_in[:, hs], st.astype(BF16))
        a = jnp.zeros((L, L), F32)
        for q_l, k_l, mask in levels:
            a = jnp.where(mask, _dot_nt(q_l[:, hs], k_l[:, hs]), a)
        o = o + _dot(a.astype(BF16), v_bf[:, hs]) + o_band[h].reshape(L, HD)
        st_sc[h] = st * s_dec[:, hs] + _dot_tn(v_bf[:, hs], k_out[:, hs])
        y = o * lax.rsqrt(jnp.mean(o * o, axis=1, keepdims=True) + EPS) * hng_ref[:, hs]
        oh_ref[:, hs] = (y * _silu(gz[:, hs])).astype(BF16)


def _merge_math(x, p, hm, oh, gab, wbrm_ref, wbrh_ref, wout_ref, wpg_ref, wple_ref, fg_ref):
    ya = _dot(hm, wbrm_ref[...])
    yb = _dot(oh, wbrh_ref[...])
    ga = gab[:, 0:D_MODEL].astype(F32)
    gb = gab[:, D_MODEL:2 * D_MODEL].astype(F32)
    y = _sigmoid(ga) * ya + _sigmoid(gb) * yb
    h1 = x + _dot(y.astype(BF16), wout_ref[...])
    pe = _dot(p.astype(BF16), wple_ref[...])
    h2 = h1 + _sigmoid(_dot(h1.astype(BF16), wpg_ref[...])) * pe
    return h2 * lax.rsqrt(jnp.mean(h2 * h2, axis=-1, keepdims=True) + EPS) * fg_ref[...]


def _hgrn_state_kernel(qf_ref, ig_ref, lb_ref, hng_ref, S0_ref, oh_ref, S_ref, st_sc, *, L, G):
    W = H_WIDTH
    for g in range(G):
        rows = pl.ds(g * L, L)
        st = st_sc.at[g]
        for h in range(H_HEADS):
            st[h] = S0_ref[g, h].T
        _hgrn_chunk(qf_ref[rows, 0:W].astype(F32), qf_ref[rows, W:2 * W].astype(F32),
                    ig_ref[rows, 0:W].astype(F32), ig_ref[rows, W:2 * W].astype(F32),
                    lb_ref, hng_ref, st, oh_ref.at[rows], L)
        for h in range(H_HEADS):
            S_ref[g, h] = st[h].T


def _hgrn_state(proj, w, S0, B, L, G):
    off = N_A // (2 * H_WIDTH)
    s_spec = pl.BlockSpec((G, H_HEADS, H_EXPAND, H_EXPAND), lambda b: (b, 0, 0, 0))
    return pl.pallas_call(
        functools.partial(_hgrn_state_kernel, L=L, G=G),
        grid=(B // G,),
        in_specs=[
            pl.BlockSpec((G * L, 2 * H_WIDTH), lambda b: (b, off)),
            pl.BlockSpec((G * L, 2 * H_WIDTH), lambda b: (b, off + 1)),
            _resident(w["hgrn_lb"].shape), _resident(w["hnorm_g"].shape), s_spec,
        ],
        out_specs=[pl.BlockSpec((G * L, H_WIDTH), lambda b: (b, 0)), s_spec],
        out_shape=[jax.ShapeDtypeStruct((B * L, H_WIDTH), BF16),
                   jax.ShapeDtypeStruct((B, H_HEADS, H_EXPAND, H_EXPAND), F32)],
        scratch_shapes=[pltpu.VMEM((G, H_HEADS, H_EXPAND, H_EXPAND), F32)],
        compiler_params=pltpu.CompilerParams(dimension_semantics=("arbitrary",), vmem_limit_bytes=VMEM_LIMIT),
        name="hgrn_state",
    )(proj, proj, w["hgrn_lb"], w["hnorm_g"], S0)


def _hgrn_kernel(x_ref, ng_ref, wb_ref, lb_ref, hng_ref, oh_ref, gab_ref, S_ref, st_sc, *, L):
    W = H_WIDTH

    @pl.when(pl.program_id(1) == 0)
    def _():
        st_sc[...] = jnp.zeros(st_sc.shape, F32)

    xh, _ = _rmsnorm_split(x_ref, ng_ref)
    q_pre = _dot_nt(xh, wb_ref[0:W, :])
    f = _dot_nt(xh, wb_ref[W:2 * W, :])
    v = _dot_nt(xh, wb_ref[2 * W:3 * W, :])
    gz = _dot_nt(xh, wb_ref[3 * W:4 * W, :])
    gab_ref[...] = _dot_nt(xh, wb_ref[4 * W:N_B, :]).astype(BF16)
    _hgrn_chunk(q_pre, f, v, gz, lb_ref, hng_ref, st_sc, oh_ref, L)

    @pl.when(pl.program_id(1) == pl.num_programs(1) - 1)
    def _():
        for h in range(H_HEADS):
            S_ref[h] = st_sc[h].T


def _hgrn(x2, w, B, T, L):
    NC = T // L
    tok = lambda b, c: (b * NC + c, 0)
    consts = [w["norm_g"], w["wt_b"], w["hgrn_lb"], w["hnorm_g"]]
    return pl.pallas_call(
        functools.partial(_hgrn_kernel, L=L),
        grid=(B, NC),
        in_specs=[pl.BlockSpec((L, D_MODEL), tok)] + [_resident(a.shape) for a in consts],
        out_specs=[pl.BlockSpec((L, H_WIDTH), tok), pl.BlockSpec((L, 2 * D_MODEL), tok),
                   pl.BlockSpec((None, H_HEADS, H_EXPAND, H_EXPAND), lambda b, c: (b, 0, 0, 0))],
        out_shape=[jax.ShapeDtypeStruct((B * T, H_WIDTH), BF16),
                   jax.ShapeDtypeStruct((B * T, 2 * D_MODEL), BF16),
                   jax.ShapeDtypeStruct((B, H_HEADS, H_EXPAND, H_EXPAND), F32)],
        scratch_shapes=[pltpu.VMEM((H_HEADS, H_EXPAND, H_EXPAND), F32)],
        compiler_params=pltpu.CompilerParams(
            dimension_semantics=("arbitrary", "arbitrary"), vmem_limit_bytes=VMEM_LIMIT),
        name="hgrn",
    )(x2, *consts)


def _merge_kernel(x_ref, p_ref, hm_ref, oh_ref, gab_ref, wbrm_ref, wbrh_ref, wout_ref, wpg_ref, wple_ref,
                  fg_ref, o_ref):
    o_ref[...] = _merge_math(x_ref[...], p_ref[...], hm_ref[...], oh_ref[...], gab_ref[...],
                             wbrm_ref, wbrh_ref, wout_ref, wpg_ref, wple_ref, fg_ref)


def _merge(x2, p2, hm, oh, gab, gab_blk, w, tm):
    n_tok = x2.shape[0]
    tok = lambda i: (i, 0)
    consts = [w["w_brm"], w["w_sq"], w["w_sq"], w["w_sq"], w["w_ple"], w["final_g"]]
    square = lambda k: pl.BlockSpec((None, D_MODEL, D_MODEL), lambda i: (k, 0, 0), pipeline_mode=pl.Buffered(1))
    return pl.pallas_call(
        _merge_kernel,
        grid=(n_tok // tm,),
        in_specs=[
            pl.BlockSpec((tm, D_MODEL), tok),
            pl.BlockSpec((tm, PLE_DIM), tok),
            pl.BlockSpec((tm, M_INNER), tok),
            pl.BlockSpec((tm, H_WIDTH), tok),
            pl.BlockSpec((tm, 2 * D_MODEL), lambda i: (i, gab_blk)),
            _resident(w["w_brm"].shape), square(0), square(1), square(2),
            _resident(w["w_ple"].shape), _resident(w["final_g"].shape),
        ],
        out_specs=pl.BlockSpec((tm, D_MODEL), tok),
        out_shape=jax.ShapeDtypeStruct((n_tok, D_MODEL), F32),
        compiler_params=pltpu.CompilerParams(
            dimension_semantics=("arbitrary",), vmem_limit_bytes=VMEM_LIMIT),
        name="merge",
    )(x2, p2, hm, oh, gab, *consts)


def _trunk(x, p, state, w):
    B, T, _ = x.shape
    x2 = x.reshape(B * T, D_MODEL)
    p2 = p.reshape(B * T, PLE_DIM)
    tm = min(MERGE_TILE, B * T)
    if state is None:
        L = min(PROMPT_CHUNK, T)
        hm, conv_n, C_n, n_n, m_n = _mlstm(x2, w, B, T, L)
        oh, gab, S_n = _hgrn(x2, w, B, T, L)
        y = _merge(x2, p2, hm, oh, gab, 0, w, tm)
    else:
        conv0, C0, n0, m0, S0 = state
        proj, gates = _proj(x2, w, min(PROJ_TILE, B * T), 2 * D_MODEL)
        m0p = jnp.zeros((B, SUBLANES, GATE_LANES), F32).at[:, :M_HEADS, :].set(m0[:, :, None])
        m_state = (conv0, C0, n0.reshape(B, M_HEADS, 1, M_HD), m0p)
        hm, conv_n, C_n, n_n, m_n = _mlstm_state(proj, gates, w, m_state, B, T, math.gcd(B, MLSTM_STREAMS))
        oh, S_n = _hgrn_state(proj, w, S0, B, T, math.gcd(B, HGRN_STREAMS))
        y = _merge(x2, p2, hm, oh, proj, (N_A + 4 * H_WIDTH) // (2 * D_MODEL), w, tm)
    return (y.reshape(B, T, D_MODEL), conv_n[None], C_n[None], n_n.reshape(B, M_HEADS, M_HD)[None],
            m_n[:, :M_HEADS, 0][None], S_n[None])


def _prep_weights(norm_g, w_in, b_ig, b_fg, conv_w, conv_b, w_qm, w_km, mnorm_g, m_skip, w_brm, hgrn_lb, hnorm_g,
                  w_brh, w_out, w_ple, w_pg, final_g):
    assert norm_g.shape[0] == 1, "single-layer trunk"
    g1 = N_A + 2 * M_HEADS
    wt = jnp.swapaxes(w_in[0], 0, 1)
    wt_gate = wt[N_A:N_A + GATE_LANES]
    wt_all = wt.astype(BF16)
    return {
        "norm_g": norm_g[0][None, :],
        "wt_all": wt_all,
        "wt_b": wt_all[g1:],
        "wgt_lo": (wt_gate - wt_gate.astype(BF16).astype(F32)).astype(BF16),
        "gbias": jnp.pad(jnp.concatenate([b_ig[0], b_fg[0]]), (0, GATE_LANES - 2 * M_HEADS))[None, :],
        "conv_w": conv_w[0],
        "conv_b": conv_b[0][None, :],
        "w_qm": w_qm[0].astype(BF16),
        "w_km": w_km[0].astype(BF16),
        "mnorm_g": mnorm_g[0][None, :],
        "m_skip": m_skip[0][None, :],
        "w_brm": w_brm[0].astype(BF16),
        "hgrn_lb": hgrn_lb,
        "hnorm_g": hnorm_g[0][None, :],
        "w_sq": jnp.stack([w_brh[0], w_out[0], w_pg[0]]).astype(BF16),
        "w_ple": w_ple[0].astype(BF16),
        "final_g": final_g[None, :],
    }


def kernel(x_prompt, x_sample, state_conv, state_mlstm_C, state_mlstm_n, state_mlstm_m, state_hgrn, p_prompt, p_sample, norm_g, w_in, b_ig, b_fg, conv_w, conv_b, w_qm, w_km, mnorm_g, m_skip, w_brm, hgrn_lb, hnorm_g, w_brh, w_out, w_ple, w_pg, final_g):
    w = _prep_weights(norm_g, w_in, b_ig, b_fg, conv_w, conv_b, w_qm, w_km, mnorm_g, m_skip, w_brm, hgrn_lb,
                      hnorm_g, w_brh, w_out, w_ple, w_pg, final_g)
    out_p = _trunk(x_prompt, p_prompt[0], None, w)
    state = (state_conv[0], state_mlstm_C[0], state_mlstm_n[0], state_mlstm_m[0], state_hgrn[0])
    out_s = _trunk(x_sample, p_sample[0], state, w)
    return (out_p[0], out_s[0]) + tuple(out_p[1:]) + tuple(out_s[1:])
```

```python
import functools
import math

import jax
import jax.numpy as jnp
from jax import lax
from jax.experimental import pallas as pl
from jax.experimental.pallas import tpu as pltpu

F32 = jnp.float32
BF16 = jnp.bfloat16

D_MODEL = 1024
PLE_DIM = 256
M_HEADS = 4
M_INNER = 2 * D_MODEL
M_HD = M_INNER // M_HEADS
CONV_W = 4
H_EXPAND = 128
H_WIDTH = D_MODEL
H_HEADS = H_WIDTH // H_EXPAND
EPS = 1e-6

N_A = 3 * M_INNER
N_B = 4 * H_WIDTH + 2 * D_MODEL
GATE_LANES = 128
SUBLANES = 8
HGRN_BAND = 4
VMEM_LIMIT = 56 * 1024 * 1024
PROMPT_CHUNK = 256
PROJ_TILE = 1024
MERGE_TILE = 512
MLSTM_STREAMS = 2
HGRN_STREAMS = 4


def _dot(a, b):
    return jnp.dot(a, b, preferred_element_type=F32)


def _dot_nt(a, b):
    return lax.dot_general(a, b, (((1,), (1,)), ((), ())), preferred_element_type=F32)


def _dot_tn(a, b):
    return lax.dot_general(a, b, (((0,), (0,)), ((), ())), preferred_element_type=F32)


def _split3(x):
    hi = x.astype(BF16)
    r = x - hi.astype(F32)
    mid = r.astype(BF16)
    lo = (r - mid.astype(F32)).astype(BF16)
    return hi, mid, lo


def _sigmoid(x):
    return 1.0 / (1.0 + jnp.exp(-x))


def _silu(x):
    return x * _sigmoid(x)


def _log_sigmoid(x):
    return jnp.minimum(x, 0.0) - jnp.log(1.0 + jnp.exp(-jnp.abs(x)))


def _tri(n, lower):
    r = lax.broadcasted_iota(jnp.int32, (n, n), 0)
    c = lax.broadcasted_iota(jnp.int32, (n, n), 1)
    return jnp.where((r >= c) if lower else (r <= c), 1.0, 0.0).astype(BF16)


def _rmsnorm_split(x_ref, g_ref):
    x = x_ref[...]
    xn = x * lax.rsqrt(jnp.mean(x * x, axis=-1, keepdims=True) + EPS) * g_ref[...]
    hi = xn.astype(BF16)
    lo = (xn - hi.astype(F32)).astype(BF16)
    return hi, lo


def _gate_dot(hi, lo, wgh_ref, wgl_ref):
    return _dot_nt(hi, wgh_ref[...]) + _dot_nt(lo, wgh_ref[...]) + _dot_nt(hi, wgl_ref[...])


def _zero_after(x):
    rows, lanes = x.shape
    m = jnp.max(x.reshape(rows // SUBLANES, SUBLANES, lanes), axis=0)
    u = pltpu.bitcast(m, jnp.uint32)
    u = lax.shift_right_logical(lax.shift_right_logical(u, jnp.uint32(16)), jnp.uint32(16))
    z = pltpu.bitcast(u, F32)
    return jnp.max(z, axis=0, keepdims=True)


def _resident(shape):
    return pl.BlockSpec(shape, lambda *_: (0,) * len(shape), pipeline_mode=pl.Buffered(1))


def _proj_kernel(x_ref, g_ref, wa_ref, wb_ref, wgh_ref, wgl_ref, o_ref, gate_ref, xh_sc, *, n_a):
    j = pl.program_id(1)

    @pl.when(j == 0)
    def _():
        hi, lo = _rmsnorm_split(x_ref, g_ref)
        xh_sc[...] = hi
        gate_ref[...] = _gate_dot(hi, lo, wgh_ref, wgl_ref)

    @pl.when(j < n_a)
    def _():
        o_ref[...] = _dot_nt(xh_sc[...], wa_ref[...]).astype(BF16)

    @pl.when(j >= n_a)
    def _():
        o_ref[...] = _dot_nt(xh_sc[...], wb_ref[...]).astype(BF16)


def _proj(x2, w, tm, tn):
    n_tok = x2.shape[0]
    n_a = N_A // tn
    return pl.pallas_call(
        functools.partial(_proj_kernel, n_a=n_a),
        grid=(n_tok // tm, (N_A + N_B) // tn),
        in_specs=[
            pl.BlockSpec((tm, D_MODEL), lambda i, j: (i, 0)),
            pl.BlockSpec((1, D_MODEL), lambda i, j: (0, 0)),
            pl.BlockSpec((tn, D_MODEL), lambda i, j: (jnp.minimum(j, n_a - 1), 0)),
            pl.BlockSpec((tn, D_MODEL), lambda i, j: (jnp.maximum(j - n_a, 0), 0)),
            pl.BlockSpec((GATE_LANES, D_MODEL), lambda i, j: (N_A // GATE_LANES, 0)),
            pl.BlockSpec((GATE_LANES, D_MODEL), lambda i, j: (0, 0)),
        ],
        out_specs=[
            pl.BlockSpec((tm, tn), lambda i, j: (i, j)),
            pl.BlockSpec((tm, GATE_LANES), lambda i, j: (i, 0)),
        ],
        out_shape=[
            jax.ShapeDtypeStruct((n_tok, N_A + N_B), BF16),
            jax.ShapeDtypeStruct((n_tok, GATE_LANES), F32),
        ],
        scratch_shapes=[pltpu.VMEM((tm, D_MODEL), BF16)],
        compiler_params=pltpu.CompilerParams(
            dimension_semantics=("arbitrary", "arbitrary"), vmem_limit_bytes=VMEM_LIMIT),
        name="proj",
    )(x2, w["norm_g"], w["wt_all"], w["wt_b"], w["wt_all"], w["wgt_lo"])


def _mlstm_conv(uext_sc, convw_ref, convb_ref, convo_ref, L):
    TAIL = CONV_W - 1
    BASE = SUBLANES

    ext = uext_sc[...]
    conv = convb_ref[...] + ext[BASE:, :] * convw_ref[TAIL:CONV_W, :]
    for j in range(TAIL):
        conv = conv + pltpu.roll(ext, TAIL - j, axis=0)[BASE:, :] * convw_ref[j:j + 1, :]
    tail = ext[BASE + L - TAIL:, :]
    uext_sc[BASE - TAIL:BASE, :] = tail
    convo_ref[...] = tail
    return _silu(conv)


def _mlstm_chunk(c_act, qk_of, gates, v_of, z_of, Cin_ref, nin_ref, min_ref, mng_ref, skip_ref, hm_ref,
                 C_ref, n_ref, m_ref, L):
    lf_cols = _log_sigmoid(gates)
    gates_t = gates.T[0:2 * M_HEADS, :]
    lf_rows = _log_sigmoid(gates_t)
    tril, triu = _tri(L, True), _tri(L, False)
    b_cols = sum(_dot(tril, part) for part in _split3(lf_cols))
    b_rows = sum(_dot(part, triu) for part in _split3(lf_rows))
    row_i = lax.broadcasted_iota(jnp.int32, (L, L), 0)
    col_i = lax.broadcasted_iota(jnp.int32, (L, L), 1)
    causal = row_i >= col_i

    done = []
    done_c = []
    for h in range(M_HEADS):
        hs = slice(h * M_HD, (h + 1) * M_HD)
        c_h = c_act[:, hs]
        ch = (c_h + done_c[h - 1] if h >= 1 else c_h).astype(BF16)
        q, k = qk_of(h, ch)
        k = k * (M_HD ** -0.5)
        v = v_of(h)
        q_bf = q.astype(BF16)
        k_bf = k.astype(BF16)

        ig_col = gates[:, h:h + 1]
        ig_row = gates_t[h:h + 1, :]
        b_col = b_cols[:, M_HEADS + h:M_HEADS + h + 1]
        b_row = b_rows[M_HEADS + h:M_HEADS + h + 1, :]
        m_c = min_ref[h:h + 1, 0:1]

        logd = jnp.where(causal, b_col + (ig_row - b_row), -jnp.inf)
        m_prev = b_col + m_c
        m_t = jnp.maximum(m_prev, jnp.max(logd, axis=1, keepdims=True))
        dmat = jnp.exp(logd - m_t)
        w_inter = jnp.exp(m_prev - m_t)

        C = Cin_ref[h]
        n = nin_ref[h]
        s_mat = _dot_nt(q_bf, k_bf) * dmat
        if h >= 1:
            s_mat = s_mat + done[h - 1][:, 0:L]
        num = w_inter * _dot_nt(q_bf, C.astype(BF16)) + _dot(s_mat.astype(BF16), v.astype(BF16))
        den = w_inter * jnp.sum(q * n, axis=1, keepdims=True) + jnp.sum(s_mat, axis=1, keepdims=True)
        hh = num * (1.0 / jnp.maximum(jnp.abs(den), jnp.exp(-m_t)))

        m_last = m_t[L - 1:L, :]
        w_last = jnp.exp(b_col[L - 1:L, :] - b_col + ig_col - m_last)
        decay = jnp.exp(m_prev[L - 1:L, :] - m_last)
        wv = (w_last * v).astype(BF16)
        c_new = decay * C + _dot_tn(wv, k_bf)
        C_ref[h] = c_new
        done_c.append(_zero_after(c_new[M_HD - M_HD // 4:, :]))
        n_ref[h] = decay * n + jnp.sum(w_last * k, axis=0, keepdims=True)
        m_ref[h:h + 1, :] = jnp.broadcast_to(m_last, (1, m_ref.shape[1]))

        mu = jnp.mean(hh, axis=1, keepdims=True)
        d = hh - mu
        y = d * lax.rsqrt(jnp.mean(d * d, axis=1, keepdims=True) + EPS) * mng_ref[:, hs]
        out = (y + skip_ref[:, hs] * c_h) * _silu(z_of(h))
        hm_ref[:, hs] = out.astype(BF16)
        done.append(_zero_after(out))


def _mlstm_state_kernel(pm_ref, gate_ref, gbias_ref, convw_ref, convb_ref, wq_ref, wk_ref, mng_ref, skip_ref,
                        conv0_ref, C0_ref, n0_ref, m0_ref, hm_ref, convo_ref, C_ref, n_ref, m_ref, uext_sc,
                        *, L, G):
    c_acts = []
    for g in range(G):
        ue = uext_sc.at[g]
        ue[SUBLANES - (CONV_W - 1):SUBLANES, :] = conv0_ref[g]
        ue[SUBLANES:SUBLANES + L, :] = pm_ref[pl.ds(g * L, L), 0:M_INNER].astype(F32)
        c_acts.append(_mlstm_conv(ue, convw_ref, convb_ref, convo_ref.at[g], L))
    c_all = jnp.concatenate(c_acts, axis=0).astype(BF16)
    q_all = [_dot(c_all[:, h * M_HD:(h + 1) * M_HD], wq_ref[h]) for h in range(M_HEADS)]
    k_all = [_dot(c_all[:, h * M_HD:(h + 1) * M_HD], wk_ref[h]) for h in range(M_HEADS)]
    for g in range(G):
        rows = pl.ds(g * L, L)
        qk_of = lambda h, c, g=g: (q_all[h][g * L:(g + 1) * L], k_all[h][g * L:(g + 1) * L])
        v_of = lambda h, rows=rows: pm_ref[rows, M_INNER + h * M_HD:M_INNER + (h + 1) * M_HD].astype(F32)
        z_of = lambda h, rows=rows: pm_ref[rows, 2 * M_INNER + h * M_HD:2 * M_INNER + (h + 1) * M_HD].astype(F32)
        _mlstm_chunk(c_acts[g], qk_of, gate_ref[rows, :] + gbias_ref[...], v_of, z_of, C0_ref.at[g], n0_ref.at[g],
                     m0_ref.at[g], mng_ref, skip_ref, hm_ref.at[rows], C_ref.at[g], n_ref.at[g], m_ref.at[g], L)
        m_ref[g, M_HEADS:, :] = m0_ref[g, M_HEADS:, :]


def _mlstm_kernel(x_ref, ng_ref, wa_ref, wgh_ref, wgl_ref, gbias_ref, convw_ref, convb_ref,
                  wq_ref, wk_ref, mng_ref, skip_ref, hm_ref, convo_ref, C_ref, n_ref, m_ref, uext_sc, *, L):
    @pl.when(pl.program_id(1) == 0)
    def _():
        uext_sc[SUBLANES - (CONV_W - 1):SUBLANES, :] = jnp.zeros((CONV_W - 1, M_INNER), F32)
        C_ref[...] = jnp.zeros(C_ref.shape, F32)
        n_ref[...] = jnp.zeros(n_ref.shape, F32)
        m_ref[...] = jnp.zeros(m_ref.shape, F32)

    xh, xl = _rmsnorm_split(x_ref, ng_ref)
    uext_sc[SUBLANES:SUBLANES + L, :] = _dot_nt(xh, wa_ref[0:M_INNER, :])
    gates = _gate_dot(xh, xl, wgh_ref, wgl_ref) + gbias_ref[...]
    v_of = lambda h: _dot_nt(xh, wa_ref[M_INNER + h * M_HD:M_INNER + (h + 1) * M_HD, :])
    z_of = lambda h: _dot_nt(xh, wa_ref[2 * M_INNER + h * M_HD:2 * M_INNER + (h + 1) * M_HD, :])
    qk_of = lambda h, c: (_dot(c, wq_ref[h]), _dot(c, wk_ref[h]))
    c_act = _mlstm_conv(uext_sc, convw_ref, convb_ref, convo_ref, L)
    _mlstm_chunk(c_act, qk_of, gates, v_of, z_of, C_ref, n_ref, m_ref, mng_ref, skip_ref, hm_ref,
                 C_ref, n_ref, m_ref, L)


def _state_specs(index, streams=None):
    return [
        pl.BlockSpec((streams, CONV_W - 1, M_INNER), lambda *g: (index(*g), 0, 0)),
        pl.BlockSpec((streams, M_HEADS, M_HD, M_HD), lambda *g: (index(*g), 0, 0, 0)),
        pl.BlockSpec((streams, M_HEADS, 1, M_HD), lambda *g: (index(*g), 0, 0, 0)),
        pl.BlockSpec((streams, SUBLANES, GATE_LANES), lambda *g: (index(*g), 0, 0)),
    ]


def _mlstm_out_shape(B, T):
    return [
        jax.ShapeDtypeStruct((B * T, M_INNER), BF16),
        jax.ShapeDtypeStruct((B, CONV_W - 1, M_INNER), F32),
        jax.ShapeDtypeStruct((B, M_HEADS, M_HD, M_HD), F32),
        jax.ShapeDtypeStruct((B, M_HEADS, 1, M_HD), F32),
        jax.ShapeDtypeStruct((B, SUBLANES, GATE_LANES), F32),
    ]


def _mlstm_shared(w):
    return [w["gbias"], w["conv_w"], w["conv_b"], w["w_qm"], w["w_km"], w["mnorm_g"], w["m_skip"]]


def _mlstm_state(proj, gates, w, state, B, L, G):
    shared = _mlstm_shared(w)
    tok = lambda b: (b, 0)
    state_specs = _state_specs(lambda b: b, streams=G)
    return pl.pallas_call(
        functools.partial(_mlstm_state_kernel, L=L, G=G),
        grid=(B // G,),
        in_specs=[pl.BlockSpec((G * L, N_A), tok), pl.BlockSpec((G * L, GATE_LANES), tok)]
        + [_resident(a.shape) for a in shared] + state_specs,
        out_specs=[pl.BlockSpec((G * L, M_INNER), tok)] + state_specs,
        out_shape=_mlstm_out_shape(B, L),
        scratch_shapes=[pltpu.VMEM((G, SUBLANES + L, M_INNER), F32)],
        compiler_params=pltpu.CompilerParams(dimension_semantics=("arbitrary",), vmem_limit_bytes=VMEM_LIMIT),
        name="mlstm_state",
    )(proj, gates, *shared, *state)


def _mlstm(x2, w, B, T, L):
    NC = T // L
    tok = lambda b, c: (b * NC + c, 0)
    consts = [w["norm_g"], w["wt_all"], w["wt_all"], w["wgt_lo"]] + _mlstm_shared(w)
    front_specs = [
        _resident(w["norm_g"].shape),
        pl.BlockSpec((N_A, D_MODEL), lambda b, c: (0, 0), pipeline_mode=pl.Buffered(1)),
        pl.BlockSpec((GATE_LANES, D_MODEL), lambda b, c: (N_A // GATE_LANES, 0), pipeline_mode=pl.Buffered(1)),
    ]
    return pl.pallas_call(
        functools.partial(_mlstm_kernel, L=L),
        grid=(B, NC),
        in_specs=[pl.BlockSpec((L, D_MODEL), tok)] + front_specs + [_resident(a.shape) for a in consts[3:]],
        out_specs=[pl.BlockSpec((L, M_INNER), tok)] + _state_specs(lambda b, c: b),
        out_shape=_mlstm_out_shape(B, T),
        scratch_shapes=[pltpu.VMEM((SUBLANES + L, M_INNER), F32)],
        compiler_params=pltpu.CompilerParams(
            dimension_semantics=("arbitrary", "arbitrary"), vmem_limit_bytes=VMEM_LIMIT),
        name="mlstm",
    )(x2, *consts)


def _hgrn_chunk(q_pre, f, v, gz, lb_ref, hng_ref, st_sc, oh_ref, L):
    HD = H_EXPAND
    W = H_WIDTH
    v_bf = v.astype(BF16)

    lbp = lb_ref[...]
    lbm = jnp.max(lbp, axis=0, keepdims=True)
    lbe = jnp.exp(lbp - lbm)
    lb = lbe[0:1, :] / jnp.sum(lbe, axis=0, keepdims=True)

    e = jnp.exp(-jnp.abs(f))
    r = 1.0 / (1.0 + e)
    sig_pos = jnp.where(f >= 0, r, e * r)
    sig_neg = jnp.where(f >= 0, e * r, r)
    fgt = lb + (1.0 - lb) * sig_pos
    lf = jnp.log(fgt)
    kk = (1.0 - lb) * sig_neg
    qa = _silu(q_pre)

    tril = _tri(L, True)
    g = sum(_dot(tril, part) for part in _split3(lf))
    g_last = g[L - 1:L, :]
    qa_bf = qa.astype(BF16)
    kk_bf = kk.astype(BF16)
    q_in = qa_bf * jnp.exp(g).astype(BF16)
    k_out = kk_bf * jnp.exp(g_last - g).astype(BF16)
    s_dec = jnp.exp(g_last)

    row_i = lax.broadcasted_iota(jnp.int32, (L, L), 0)
    col_i = lax.broadcasted_iota(jnp.int32, (L, L), 1)
    levels = []
    hb = HGRN_BAND
    while hb < L:
        nb = L // (2 * hb)
        g3 = g.reshape(nb, 2 * hb, W)
        ge = jnp.broadcast_to(g3[:, hb - 1:hb, :], (nb, 2 * hb, W)).reshape(L, W)
        e_l = jnp.exp(-jnp.abs(g - ge)).astype(BF16)
        q_l = qa_bf * e_l
        k_l = kk_bf * e_l
        sh = (2 * hb).bit_length() - 1
        same_blk = jnp.right_shift(row_i, sh) == jnp.right_shift(col_i, sh)
        mask = same_blk & (jnp.bitwise_and(row_i, hb) != 0) & (jnp.bitwise_and(col_i, hb) == 0)
        levels.append((q_l, k_l, mask))
        hb *= 2

    nblk = L // SUBLANES
    to_blocks = lambda a: a.reshape(nblk, SUBLANES, W)
    sub_i = lax.broadcasted_iota(jnp.int32, (nblk, SUBLANES, W), 1)
    f0 = jnp.where(jnp.bitwise_and(sub_i, HGRN_BAND - 1) == 0, 0.0, to_blocks(fgt))
    qa3 = to_blocks(qa)
    ke = to_blocks(kk)
    vs = to_blocks(v)
    o_band = [None] * H_HEADS
    for d in range(HGRN_BAND):
        if d > 0:
            ke = f0 * pltpu.roll(ke, 1, axis=1)
            vs = pltpu.roll(vs, 1, axis=1)
        p_d = qa3 * ke
        for h in range(H_HEADS):
            hs = slice(h * HD, (h + 1) * HD)
            term = jnp.sum(p_d[:, :, hs], axis=2, keepdims=True) * vs[:, :, hs]
            o_band[h] = term if d == 0 else o_band[h] + term

    for h in range(H_HEADS):
        hs = slice(h * HD, (h + 1) * HD)
        st = st_sc[h]
        o = _dot_nt(q_in[:, hs], st.astype(BF16))
        a = jnp.zeros((L, L), F32)
        for q_l, k_l, mask in levels:
            a = jnp.where(mask, _dot_nt(q_l[:, hs], k_l[:, hs]), a)
        o = o + _dot(a.astype(BF16), v_bf[:, hs]) + o_band[h].reshape(L, HD)
        st_sc[h] = st * s_dec[:, hs] + _dot_tn(v_bf[:, hs], k_out[:, hs])
        y = o * lax.rsqrt(jnp.mean(o * o, axis=1, keepdims=True) + EPS) * hng_ref[:, hs]
        oh_ref[:, hs] = (y * _silu(gz[:, hs])).astype(BF16)


def _merge_math(x, p, hm, oh, gab, wbrm_ref, wbrh_ref, wout_ref, wpg_ref, wple_ref, fg_ref):
    ya = _dot(hm, wbrm_ref[...])
    yb = _dot(oh, wbrh_ref[...])
    ga = gab[:, 0:D_MODEL].astype(F32)
    gb = gab[:, D_MODEL:2 * D_MODEL].astype(F32)
    y = _sigmoid(ga) * ya + _sigmoid(gb) * yb
    h1 = x + _dot(y.astype(BF16), wout_ref[...])
    pe = _dot(p.astype(BF16), wple_ref[...])
    h2 = h1 + _sigmoid(_dot(h1.astype(BF16), wpg_ref[...])) * pe
    return h2 * lax.rsqrt(jnp.mean(h2 * h2, axis=-1, keepdims=True) + EPS) * fg_ref[...]


def _hgrn_state_kernel(qf_ref, ig_ref, lb_ref, hng_ref, S0_ref, oh_ref, S_ref, st_sc, *, L, G):
    W = H_WIDTH
    for g in range(G):
        rows = pl.ds(g * L, L)
        st = st_sc.at[g]
        for h in range(H_HEADS):
            st[h] = S0_ref[g, h].T
        _hgrn_chunk(qf_ref[rows, 0:W].astype(F32), qf_ref[rows, W:2 * W].astype(F32),
                    ig_ref[rows, 0:W].astype(F32), ig_ref[rows, W:2 * W].astype(F32),
                    lb_ref, hng_ref, st, oh_ref.at[rows], L)
        for h in range(H_HEADS):
            S_ref[g, h] = st[h].T


def _hgrn_state(proj, w, S0, B, L, G):
    off = N_A // (2 * H_WIDTH)
    s_spec = pl.BlockSpec((G, H_HEADS, H_EXPAND, H_EXPAND), lambda b: (b, 0, 0, 0))
    return pl.pallas_call(
        functools.partial(_hgrn_state_kernel, L=L, G=G),
        grid=(B // G,),
        in_specs=[
            pl.BlockSpec((G * L, 2 * H_WIDTH), lambda b: (b, off)),
            pl.BlockSpec((G * L, 2 * H_WIDTH), lambda b: (b, off + 1)),
            _resident(w["hgrn_lb"].shape), _resident(w["hnorm_g"].shape), s_spec,
        ],
        out_specs=[pl.BlockSpec((G * L, H_WIDTH), lambda b: (b, 0)), s_spec],
        out_shape=[jax.ShapeDtypeStruct((B * L, H_WIDTH), BF16),
                   jax.ShapeDtypeStruct((B, H_HEADS, H_EXPAND, H_EXPAND), F32)],
        scratch_shapes=[pltpu.VMEM((G, H_HEADS, H_EXPAND, H_EXPAND), F32)],
        compiler_params=pltpu.CompilerParams(dimension_semantics=("arbitrary",), vmem_limit_bytes=VMEM_LIMIT),
        name="hgrn_state",
    )(proj, proj, w["hgrn_lb"], w["hnorm_g"], S0)


def _hgrn_kernel(x_ref, ng_ref, wb_ref, lb_ref, hng_ref, oh_ref, gab_ref, S_ref, st_sc, *, L):
    W = H_WIDTH

    @pl.when(pl.program_id(1) == 0)
    def _():
        st_sc[...] = jnp.zeros(st_sc.shape, F32)

    xh, _ = _rmsnorm_split(x_ref, ng_ref)
    q_pre = _dot_nt(xh, wb_ref[0:W, :])
    f = _dot_nt(xh, wb_ref[W:2 * W, :])
    v = _dot_nt(xh, wb_ref[2 * W:3 * W, :])
    gz = _dot_nt(xh, wb_ref[3 * W:4 * W, :])
    gab_ref[...] = _dot_nt(xh, wb_ref[4 * W:N_B, :]).astype(BF16)
    _hgrn_chunk(q_pre, f, v, gz, lb_ref, hng_ref, st_sc, oh_ref, L)

    @pl.when(pl.program_id(1) == pl.num_programs(1) - 1)
    def _():
        for h in range(H_HEADS):
            S_ref[h] = st_sc[h].T


def _hgrn(x2, w, B, T, L):
    NC = T // L
    tok = lambda b, c: (b * NC + c, 0)
    consts = [w["norm_g"], w["wt_b"], w["hgrn_lb"], w["hnorm_g"]]
    return pl.pallas_call(
        functools.partial(_hgrn_kernel, L=L),
        grid=(B, NC),
        in_specs=[pl.BlockSpec((L, D_MODEL), tok)] + [_resident(a.shape) for a in consts],
        out_specs=[pl.BlockSpec((L, H_WIDTH), tok), pl.BlockSpec((L, 2 * D_MODEL), tok),
                   pl.BlockSpec((None, H_HEADS, H_EXPAND, H_EXPAND), lambda b, c: (b, 0, 0, 0))],
        out_shape=[jax.ShapeDtypeStruct((B * T, H_WIDTH), BF16),
                   jax.ShapeDtypeStruct((B * T, 2 * D_MODEL), BF16),
                   jax.ShapeDtypeStruct((B, H_HEADS, H_EXPAND, H_EXPAND), F32)],
        scratch_shapes=[pltpu.VMEM((H_HEADS, H_EXPAND, H_EXPAND), F32)],
        compiler_params=pltpu.CompilerParams(
            dimension_semantics=("arbitrary", "arbitrary"), vmem_limit_bytes=VMEM_LIMIT),
        name="hgrn",
    )(x2, *consts)


def _merge_kernel(x_ref, p_ref, hm_ref, oh_ref, gab_ref, wbrm_ref, wbrh_ref, wout_ref, wpg_ref, wple_ref,
                  fg_ref, o_ref):
    o_ref[...] = _merge_math(x_ref[...], p_ref[...], hm_ref[...], oh_ref[...], gab_ref[...],
                             wbrm_ref, wbrh_ref, wout_ref, wpg_ref, wple_ref, fg_ref)


def _merge(x2, p2, hm, oh, gab, gab_blk, w, tm):
    n_tok = x2.shape[0]
    tok = lambda i: (i, 0)
    consts = [w["w_brm"], w["w_sq"], w["w_sq"], w["w_sq"], w["w_ple"], w["final_g"]]
    square = lambda k: pl.BlockSpec((None, D_MODEL, D_MODEL), lambda i: (k, 0, 0), pipeline_mode=pl.Buffered(1))
    return pl.pallas_call(
        _merge_kernel,
        grid=(n_tok // tm,),
        in_specs=[
            pl.BlockSpec((tm, D_MODEL), tok),
            pl.BlockSpec((tm, PLE_DIM), tok),
            pl.BlockSpec((tm, M_INNER), tok),
            pl.BlockSpec((tm, H_WIDTH), tok),
            pl.BlockSpec((tm, 2 * D_MODEL), lambda i: (i, gab_blk)),
            _resident(w["w_brm"].shape), square(0), square(1), square(2),
            _resident(w["w_ple"].shape), _resident(w["final_g"].shape),
        ],
        out_specs=pl.BlockSpec((tm, D_MODEL), tok),
        out_shape=jax.ShapeDtypeStruct((n_tok, D_MODEL), F32),
        compiler_params=pltpu.CompilerParams(
            dimension_semantics=("arbitrary",), vmem_limit_bytes=VMEM_LIMIT),
        name="merge",
    )(x2, p2, hm, oh, gab, *consts)


def _trunk(x, p, state, w):
    B, T, _ = x.shape
    x2 = x.reshape(B * T, D_MODEL)
    p2 = p.reshape(B * T, PLE_DIM)
    tm = min(MERGE_TILE, B * T)
    if state is None:
        L = min(PROMPT_CHUNK, T)
        hm, conv_n, C_n, n_n, m_n = _mlstm(x2, w, B, T, L)
        oh, gab, S_n = _hgrn(x2, w, B, T, L)
        y = _merge(x2, p2, hm, oh, gab, 0, w, tm)
    else:
        conv0, C0, n0, m0, S0 = state
        proj, gates = _proj(x2, w, min(PROJ_TILE, B * T), 2 * D_MODEL)
        m0p = jnp.zeros((B, SUBLANES, GATE_LANES), F32).at[:, :M_HEADS, :].set(m0[:, :, None])
        m_state = (conv0, C0, n0.reshape(B, M_HEADS, 1, M_HD), m0p)
        hm, conv_n, C_n, n_n, m_n = _mlstm_state(proj, gates, w, m_state, B, T, math.gcd(B, MLSTM_STREAMS))
        oh, S_n = _hgrn_state(proj, w, S0, B, T, math.gcd(B, HGRN_STREAMS))
        y = _merge(x2, p2, hm, oh, proj, (N_A + 4 * H_WIDTH) // (2 * D_MODEL), w, tm)
    return (y.reshape(B, T, D_MODEL), conv_n[None], C_n[None], n_n.reshape(B, M_HEADS, M_HD)[None],
            m_n[:, :M_HEADS, 0][None], S_n[None])


def _prep_weights(norm_g, w_in, b_ig, b_fg, conv_w, conv_b, w_qm, w_km, mnorm_g, m_skip, w_brm, hgrn_lb, hnorm_g,
                  w_brh, w_out, w_ple, w_pg, final_g):
    assert norm_g.shape[0] == 1, "single-layer trunk"
    g1 = N_A + 2 * M_HEADS
    wt = jnp.swapaxes(w_in[0], 0, 1)
    wt_gate = wt[N_A:N_A + GATE_LANES]
    wt_all = wt.astype(BF16)
    return {
        "norm_g": norm_g[0][None, :],
        "wt_all": wt_all,
        "wt_b": wt_all[g1:],
        "wgt_lo": (wt_gate - wt_gate.astype(BF16).astype(F32)).astype(BF16),
        "gbias": jnp.pad(jnp.concatenate([b_ig[0], b_fg[0]]), (0, GATE_LANES - 2 * M_HEADS))[None, :],
        "conv_w": conv_w[0],
        "conv_b": conv_b[0][None, :],
        "w_qm": w_qm[0].astype(BF16),
        "w_km": w_km[0].astype(BF16),
        "mnorm_g": mnorm_g[0][None, :],
        "m_skip": m_skip[0][None, :],
        "w_brm": w_brm[0].astype(BF16),
        "hgrn_lb": hgrn_lb,
        "hnorm_g": hnorm_g[0][None, :],
        "w_sq": jnp.stack([w_brh[0], w_out[0], w_pg[0]]).astype(BF16),
        "w_ple": w_ple[0].astype(BF16),
        "final_g": final_g[None, :],
    }


def kernel(x_prompt, x_sample, state_conv, state_mlstm_C, state_mlstm_n, state_mlstm_m, state_hgrn, p_prompt, p_sample, norm_g, w_in, b_ig, b_fg, conv_w, conv_b, w_qm, w_km, mnorm_g, m_skip, w_brm, hgrn_lb, hnorm_g, w_brh, w_out, w_ple, w_pg, final_g):
    w = _prep_weights(norm_g, w_in, b_ig, b_fg, conv_w, conv_b, w_qm, w_km, mnorm_g, m_skip, w_brm, hgrn_lb,
                      hnorm_g, w_brh, w_out, w_ple, w_pg, final_g)
    out_p = _trunk(x_prompt, p_prompt[0], None, w)
    state = (state_conv[0], state_mlstm_C[0], state_mlstm_n[0], state_mlstm_m[0], state_hgrn[0])
    out_s = _trunk(x_sample, p_sample[0], state, w)
    return (out_p[0], out_s[0]) + tuple(out_p[1:]) + tuple(out_s[1:])
```

```python
import functools
import math

import jax
import jax.numpy as jnp
from jax import lax
from jax.experimental import pallas as pl
from jax.experimental.pallas import tpu as pltpu

F32 = jnp.float32
BF16 = jnp.bfloat16

D_MODEL = 1024
PLE_DIM = 256
M_HEADS = 4
M_INNER = 2 * D_MODEL
M_HD = M_INNER // M_HEADS
CONV_W = 4
H_EXPAND = 128
H_WIDTH = D_MODEL
H_HEADS = H_WIDTH // H_EXPAND
EPS = 1e-6

N_A = 3 * M_INNER
N_B = 4 * H_WIDTH + 2 * D_MODEL
GATE_LANES = 128
SUBLANES = 8
HGRN_BAND = 4
VMEM_LIMIT = 56 * 1024 * 1024
PROMPT_CHUNK = 256
PROJ_TILE = 1024
MERGE_TILE = 512
MLSTM_STREAMS = 2
HGRN_STREAMS = 4


def _dot(a, b):
    return jnp.dot(a, b, preferred_element_type=F32)


def _dot_nt(a, b):
    return lax.dot_general(a, b, (((1,), (1,)), ((), ())), preferred_element_type=F32)


def _dot_tn(a, b):
    return lax.dot_general(a, b, (((0,), (0,)), ((), ())), preferred_element_type=F32)


def _split3(x):
    hi = x.astype(BF16)
    r = x - hi.astype(F32)
    mid = r.astype(BF16)
    lo = (r - mid.astype(F32)).astype(BF16)
    return hi, mid, lo


def _sigmoid(x):
    return 1.0 / (1.0 + jnp.exp(-x))


def _silu(x):
    return x * _sigmoid(x)


def _log_sigmoid(x):
    return jnp.minimum(x, 0.0) - jnp.log(1.0 + jnp.exp(-jnp.abs(x)))


def _tri(n, lower):
    r = lax.broadcasted_iota(jnp.int32, (n, n), 0)
    c = lax.broadcasted_iota(jnp.int32, (n, n), 1)
    return jnp.where((r >= c) if lower else (r <= c), 1.0, 0.0).astype(BF16)


def _rmsnorm_split(x_ref, g_ref):
    x = x_ref[...]
    xn = x * lax.rsqrt(jnp.mean(x * x, axis=-1, keepdims=True) + EPS) * g_ref[...]
    hi = xn.astype(BF16)
    lo = (xn - hi.astype(F32)).astype(BF16)
    return hi, lo


def _gate_dot(hi, lo, wgh_ref, wgl_ref):
    return _dot_nt(hi, wgh_ref[...]) + _dot_nt(lo, wgh_ref[...]) + _dot_nt(hi, wgl_ref[...])


def _zero_after(x):
    rows, lanes = x.shape
    m = jnp.max(x.reshape(rows // SUBLANES, SUBLANES, lanes), axis=0)
    u = pltpu.bitcast(m, jnp.uint32)
    u = lax.shift_right_logical(lax.shift_right_logical(u, jnp.uint32(16)), jnp.uint32(16))
    z = pltpu.bitcast(u, F32)
    return jnp.max(z, axis=0, keepdims=True)


def _resident(shape):
    return pl.BlockSpec(shape, lambda *_: (0,) * len(shape), pipeline_mode=pl.Buffered(1))


def _proj_kernel(x_ref, g_ref, wa_ref, wb_ref, wgh_ref, wgl_ref, o_ref, gate_ref, xh_sc, *, n_a):
    j = pl.program_id(1)

    @pl.when(j == 0)
    def _():
        hi, lo = _rmsnorm_split(x_ref, g_ref)
        xh_sc[...] = hi
        gate_ref[...] = _gate_dot(hi, lo, wgh_ref, wgl_ref)

    @pl.when(j < n_a)
    def _():
        o_ref[...] = _dot_nt(xh_sc[...], wa_ref[...]).astype(BF16)

    @pl.when(j >= n_a)
    def _():
        o_ref[...] = _dot_nt(xh_sc[...], wb_ref[...]).astype(BF16)


def _proj(x2, w, tm, tn):
    n_tok = x2.shape[0]
    n_a = N_A // tn
    return pl.pallas_call(
        functools.partial(_proj_kernel, n_a=n_a),
        grid=(n_tok // tm, (N_A + N_B) // tn),
        in_specs=[
            pl.BlockSpec((tm, D_MODEL), lambda i, j: (i, 0)),
            pl.BlockSpec((1, D_MODEL), lambda i, j: (0, 0)),
            pl.BlockSpec((tn, D_MODEL), lambda i, j: (jnp.minimum(j, n_a - 1), 0)),
            pl.BlockSpec((tn, D_MODEL), lambda i, j: (jnp.maximum(j - n_a, 0), 0)),
            pl.BlockSpec((GATE_LANES, D_MODEL), lambda i, j: (N_A // GATE_LANES, 0)),
            pl.BlockSpec((GATE_LANES, D_MODEL), lambda i, j: (0, 0)),
        ],
        out_specs=[
            pl.BlockSpec((tm, tn), lambda i, j: (i, j)),
            pl.BlockSpec((tm, GATE_LANES), lambda i, j: (i, 0)),
        ],
        out_shape=[
            jax.ShapeDtypeStruct((n_tok, N_A + N_B), BF16),
            jax.ShapeDtypeStruct((n_tok, GATE_LANES), F32),
        ],
        scratch_shapes=[pltpu.VMEM((tm, D_MODEL), BF16)],
        compiler_params=pltpu.CompilerParams(
            dimension_semantics=("arbitrary", "arbitrary"), vmem_limit_bytes=VMEM_LIMIT),
        name="proj",
    )(x2, w["norm_g"], w["wt_all"], w["wt_b"], w["wt_all"], w["wgt_lo"])


def _mlstm_conv(uext_sc, convw_ref, convb_ref, convo_ref, L):
    TAIL = CONV_W - 1
    BASE = SUBLANES

    ext = uext_sc[...]
    conv = convb_ref[...] + ext[BASE:, :] * convw_ref[TAIL:CONV_W, :]
    for j in range(TAIL):
        conv = conv + pltpu.roll(ext, TAIL - j, axis=0)[BASE:, :] * convw_ref[j:j + 1, :]
    tail = ext[BASE + L - TAIL:, :]
    uext_sc[BASE - TAIL:BASE, :] = tail
    convo_ref[...] = tail
    return _silu(conv)


def _mlstm_chunk(c_act, qk_of, gates, v_of, z_of, Cin_ref, nin_ref, min_ref, mng_ref, skip_ref, hm_ref,
                 C_ref, n_ref, m_ref, L):
    lf_cols = _log_sigmoid(gates)
    gates_t = gates.T[0:2 * M_HEADS, :]
    lf_rows = _log_sigmoid(gates_t)
    tril, triu = _tri(L, True), _tri(L, False)
    b_cols = sum(_dot(tril, part) for part in _split3(lf_cols))
    b_rows = sum(_dot(part, triu) for part in _split3(lf_rows))
    row_i = lax.broadcasted_iota(jnp.int32, (L, L), 0)
    col_i = lax.broadcasted_iota(jnp.int32, (L, L), 1)
    causal = row_i >= col_i

    done = []
    done_c = []
    for h in range(M_HEADS):
        hs = slice(h * M_HD, (h + 1) * M_HD)
        c_h = c_act[:, hs]
        ch = (c_h + done_c[h - 1] if h >= 1 else c_h).astype(BF16)
        q, k = qk_of(h, ch)
        k = k * (M_HD ** -0.5)
        v = v_of(h)
        q_bf = q.astype(BF16)
        k_bf = k.astype(BF16)

        ig_col = gates[:, h:h + 1]
        ig_row = gates_t[h:h + 1, :]
        b_col = b_cols[:, M_HEADS + h:M_HEADS + h + 1]
        b_row = b_rows[M_HEADS + h:M_HEADS + h + 1, :]
        m_c = min_ref[h:h + 1, 0:1]

        logd = jnp.where(causal, b_col + (ig_row - b_row), -jnp.inf)
        m_prev = b_col + m_c
        m_t = jnp.maximum(m_prev, jnp.max(logd, axis=1, keepdims=True))
        dmat = jnp.exp(logd - m_t)
        w_inter = jnp.exp(m_prev - m_t)

        C = Cin_ref[h]
        n = nin_ref[h]
        s_mat = _dot_nt(q_bf, k_bf) * dmat
        if h >= 1:
            s_mat = s_mat + done[h - 1][:, 0:L]
        num = w_inter * _dot_nt(q_bf, C.astype(BF16)) + _dot(s_mat.astype(BF16), v.astype(BF16))
        den = w_inter * jnp.sum(q * n, axis=1, keepdims=True) + jnp.sum(s_mat, axis=1, keepdims=True)
        hh = num * (1.0 / jnp.maximum(jnp.abs(den), jnp.exp(-m_t)))

        m_last = m_t[L - 1:L, :]
        w_last = jnp.exp(b_col[L - 1:L, :] - b_col + ig_col - m_last)
        decay = jnp.exp(m_prev[L - 1:L, :] - m_last)
        wv = (w_last * v).astype(BF16)
        c_new = decay * C + _dot_tn(wv, k_bf)
        C_ref[h] = c_new
        done_c.append(_zero_after(c_new[M_HD - M_HD // 4:, :]))
        n_ref[h] = decay * n + jnp.sum(w_last * k, axis=0, keepdims=True)
        m_ref[h:h + 1, :] = jnp.broadcast_to(m_last, (1, m_ref.shape[1]))

        mu = jnp.mean(hh, axis=1, keepdims=True)
        d = hh - mu
        y = d * lax.rsqrt(jnp.mean(d * d, axis=1, keepdims=True) + EPS) * mng_ref[:, hs]
        out = (y + skip_ref[:, hs] * c_h) * _silu(z_of(h))
        hm_ref[:, hs] = out.astype(BF16)
        done.append(_zero_after(out))


def _mlstm_state_kernel(pm_ref, gate_ref, gbias_ref, convw_ref, convb_ref, wq_ref, wk_ref, mng_ref, skip_ref,
                        conv0_ref, C0_ref, n0_ref, m0_ref, hm_ref, convo_ref, C_ref, n_ref, m_ref, uext_sc,
                        *, L, G):
    c_acts = []
    for g in range(G):
        ue = uext_sc.at[g]
        ue[SUBLANES - (CONV_W - 1):SUBLANES, :] = conv0_ref[g]
        ue[SUBLANES:SUBLANES + L, :] = pm_ref[pl.ds(g * L, L), 0:M_INNER].astype(F32)
        c_acts.append(_mlstm_conv(ue, convw_ref, convb_ref, convo_ref.at[g], L))
    c_all = jnp.concatenate(c_acts, axis=0).astype(BF16)
    q_all = [_dot(c_all[:, h * M_HD:(h + 1) * M_HD], wq_ref[h]) for h in range(M_HEADS)]
    k_all = [_dot(c_all[:, h * M_HD:(h + 1) * M_HD], wk_ref[h]) for h in range(M_HEADS)]
    for g in range(G):
        rows = pl.ds(g * L, L)
        qk_of = lambda h, c, g=g: (q_all[h][g * L:(g + 1) * L] + _zero_after(c.astype(F32)),
                                   k_all[h][g * L:(g + 1) * L])
        v_of = lambda h, rows=rows: pm_ref[rows, M_INNER + h * M_HD:M_INNER + (h + 1) * M_HD].astype(F32)
        z_of = lambda h, rows=rows: pm_ref[rows, 2 * M_INNER + h * M_HD:2 * M_INNER + (h + 1) * M_HD].astype(F32)
        _mlstm_chunk(c_acts[g], qk_of, gate_ref[rows, :] + gbias_ref[...], v_of, z_of, C0_ref.at[g], n0_ref.at[g],
                     m0_ref.at[g], mng_ref, skip_ref, hm_ref.at[rows], C_ref.at[g], n_ref.at[g], m_ref.at[g], L)
        m_ref[g, M_HEADS:, :] = m0_ref[g, M_HEADS:, :]


def _mlstm_kernel(x_ref, ng_ref, wa_ref, wgh_ref, wgl_ref, gbias_ref, convw_ref, convb_ref,
                  wq_ref, wk_ref, mng_ref, skip_ref, hm_ref, convo_ref, C_ref, n_ref, m_ref, uext_sc, *, L):
    @pl.when(pl.program_id(1) == 0)
    def _():
        uext_sc[SUBLANES - (CONV_W - 1):SUBLANES, :] = jnp.zeros((CONV_W - 1, M_INNER), F32)
        C_ref[...] = jnp.zeros(C_ref.shape, F32)
        n_ref[...] = jnp.zeros(n_ref.shape, F32)
        m_ref[...] = jnp.zeros(m_ref.shape, F32)

    xh, xl = _rmsnorm_split(x_ref, ng_ref)
    uext_sc[SUBLANES:SUBLANES + L, :] = _dot_nt(xh, wa_ref[0:M_INNER, :])
    gates = _gate_dot(xh, xl, wgh_ref, wgl_ref) + gbias_ref[...]
    v_of = lambda h: _dot_nt(xh, wa_ref[M_INNER + h * M_HD:M_INNER + (h + 1) * M_HD, :])
    z_of = lambda h: _dot_nt(xh, wa_ref[2 * M_INNER + h * M_HD:2 * M_INNER + (h + 1) * M_HD, :])
    qk_of = lambda h, c: (_dot(c, wq_ref[h]), _dot(c, wk_ref[h]))
    c_act = _mlstm_conv(uext_sc, convw_ref, convb_ref, convo_ref, L)
    _mlstm_chunk(c_act, qk_of, gates, v_of, z_of, C_ref, n_ref, m_ref, mng_ref, skip_ref, hm_ref,
                 C_ref, n_ref, m_ref, L)


def _state_specs(index, streams=None):
    return [
        pl.BlockSpec((streams, CONV_W - 1, M_INNER), lambda *g: (index(*g), 0, 0)),
        pl.BlockSpec((streams, M_HEADS, M_HD, M_HD), lambda *g: (index(*g), 0, 0, 0)),
        pl.BlockSpec((streams, M_HEADS, 1, M_HD), lambda *g: (index(*g), 0, 0, 0)),
        pl.BlockSpec((streams, SUBLANES, GATE_LANES), lambda *g: (index(*g), 0, 0)),
    ]


def _mlstm_out_shape(B, T):
    return [
        jax.ShapeDtypeStruct((B * T, M_INNER), BF16),
        jax.ShapeDtypeStruct((B, CONV_W - 1, M_INNER), F32),
        jax.ShapeDtypeStruct((B, M_HEADS, M_HD, M_HD), F32),
        jax.ShapeDtypeStruct((B, M_HEADS, 1, M_HD), F32),
        jax.ShapeDtypeStruct((B, SUBLANES, GATE_LANES), F32),
    ]


def _mlstm_shared(w):
    return [w["gbias"], w["conv_w"], w["conv_b"], w["w_qm"], w["w_km"], w["mnorm_g"], w["m_skip"]]


def _mlstm_state(proj, gates, w, state, B, L, G):
    shared = _mlstm_shared(w)
    tok = lambda b: (b, 0)
    state_specs = _state_specs(lambda b: b, streams=G)
    return pl.pallas_call(
        functools.partial(_mlstm_state_kernel, L=L, G=G),
        grid=(B // G,),
        in_specs=[pl.BlockSpec((G * L, N_A), tok), pl.BlockSpec((G * L, GATE_LANES), tok)]
        + [_resident(a.shape) for a in shared] + state_specs,
        out_specs=[pl.BlockSpec((G * L, M_INNER), tok)] + state_specs,
        out_shape=_mlstm_out_shape(B, L),
        scratch_shapes=[pltpu.VMEM((G, SUBLANES + L, M_INNER), F32)],
        compiler_params=pltpu.CompilerParams(dimension_semantics=("arbitrary",), vmem_limit_bytes=VMEM_LIMIT),
        name="mlstm_state",
    )(proj, gates, *shared, *state)


def _mlstm(x2, w, B, T, L):
    NC = T // L
    tok = lambda b, c: (b * NC + c, 0)
    consts = [w["norm_g"], w["wt_all"], w["wt_all"], w["wgt_lo"]] + _mlstm_shared(w)
    front_specs = [
        _resident(w["norm_g"].shape),
        pl.BlockSpec((N_A, D_MODEL), lambda b, c: (0, 0), pipeline_mode=pl.Buffered(1)),
        pl.BlockSpec((GATE_LANES, D_MODEL), lambda b, c: (N_A // GATE_LANES, 0), pipeline_mode=pl.Buffered(1)),
    ]
    return pl.pallas_call(
        functools.partial(_mlstm_kernel, L=L),
        grid=(B, NC),
        in_specs=[pl.BlockSpec((L, D_MODEL), tok)] + front_specs + [_resident(a.shape) for a in consts[3:]],
        out_specs=[pl.BlockSpec((L, M_INNER), tok)] + _state_specs(lambda b, c: b),
        out_shape=_mlstm_out_shape(B, T),
        scratch_shapes=[pltpu.VMEM((SUBLANES + L, M_INNER), F32)],
        compiler_params=pltpu.CompilerParams(
            dimension_semantics=("arbitrary", "arbitrary"), vmem_limit_bytes=VMEM_LIMIT),
        name="mlstm",
    )(x2, *consts)


def _hgrn_chunk(q_pre, f, v, gz, lb_ref, hng_ref, st_sc, oh_ref, L):
    HD = H_EXPAND
    W = H_WIDTH
    v_bf = v.astype(BF16)

    lbp = lb_ref[...]
    lbm = jnp.max(lbp, axis=0, keepdims=True)
    lbe = jnp.exp(lbp - lbm)
    lb = lbe[0:1, :] / jnp.sum(lbe, axis=0, keepdims=True)

    e = jnp.exp(-jnp.abs(f))
    r = 1.0 / (1.0 + e)
    sig_pos = jnp.where(f >= 0, r, e * r)
    sig_neg = jnp.where(f >= 0, e * r, r)
    fgt = lb + (1.0 - lb) * sig_pos
    lf = jnp.log(fgt)
    kk = (1.0 - lb) * sig_neg
    qa = _silu(q_pre)

    tril = _tri(L, True)
    g = sum(_dot(tril, part) for part in _split3(lf))
    g_last = g[L - 1:L, :]
    qa_bf = qa.astype(BF16)
    kk_bf = kk.astype(BF16)
    q_in = qa_bf * jnp.exp(g).astype(BF16)
    k_out = kk_bf * jnp.exp(g_last - g).astype(BF16)
    s_dec = jnp.exp(g_last)

    row_i = lax.broadcasted_iota(jnp.int32, (L, L), 0)
    col_i = lax.broadcasted_iota(jnp.int32, (L, L), 1)
    levels = []
    hb = HGRN_BAND
    while hb < L:
        nb = L // (2 * hb)
        g3 = g.reshape(nb, 2 * hb, W)
        ge = jnp.broadcast_to(g3[:, hb - 1:hb, :], (nb, 2 * hb, W)).reshape(L, W)
        e_l = jnp.exp(-jnp.abs(g - ge)).astype(BF16)
        q_l = qa_bf * e_l
        k_l = kk_bf * e_l
        sh = (2 * hb).bit_length() - 1
        same_blk = jnp.right_shift(row_i, sh) == jnp.right_shift(col_i, sh)
        mask = same_blk & (jnp.bitwise_and(row_i, hb) != 0) & (jnp.bitwise_and(col_i, hb) == 0)
        levels.append((q_l, k_l, mask))
        hb *= 2

    nblk = L // SUBLANES
    to_blocks = lambda a: a.reshape(nblk, SUBLANES, W)
    sub_i = lax.broadcasted_iota(jnp.int32, (nblk, SUBLANES, W), 1)
    f0 = jnp.where(jnp.bitwise_and(sub_i, HGRN_BAND - 1) == 0, 0.0, to_blocks(fgt))
    qa3 = to_blocks(qa)
    ke = to_blocks(kk)
    vs = to_blocks(v)
    o_band = [None] * H_HEADS
    for d in range(HGRN_BAND):
        if d > 0:
            ke = f0 * pltpu.roll(ke, 1, axis=1)
            vs = pltpu.roll(vs, 1, axis=1)
        p_d = qa3 * ke
        for h in range(H_HEADS):
            hs = slice(h * HD, (h + 1) * HD)
            term = jnp.sum(p_d[:, :, hs], axis=2, keepdims=True) * vs[:, :, hs]
            o_band[h] = term if d == 0 else o_band[h] + term

    for h in range(H_HEADS):
        hs = slice(h * HD, (h + 1) * HD)
        st = st_sc[h]
        o = _dot_nt(q_in[:, hs], st.astype(BF16))
        a = jnp.zeros((L, L), F32)
        for q_l, k_l, mask in levels:
            a = jnp.where(mask, _dot_nt(q_l[:, hs], k_l[:, hs]), a)
        o = o + _dot(a.astype(BF16), v_bf[:, hs]) + o_band[h].reshape(L, HD)
        st_sc[h] = st * s_dec[:, hs] + _dot_tn(v_bf[:, hs], k_out[:, hs])
        y = o * lax.rsqrt(jnp.mean(o * o, axis=1, keepdims=True) + EPS) * hng_ref[:, hs]
        oh_ref[:, hs] = (y * _silu(gz[:, hs])).astype(BF16)


def _merge_math(x, p, hm, oh, gab, wbrm_ref, wbrh_ref, wout_ref, wpg_ref, wple_ref, fg_ref):
    ya = _dot(hm, wbrm_ref[...])
    yb = _dot(oh, wbrh_ref[...])
    ga = gab[:, 0:D_MODEL].astype(F32)
    gb = gab[:, D_MODEL:2 * D_MODEL].astype(F32)
    y = _sigmoid(ga) * ya + _sigmoid(gb) * yb
    h1 = x + _dot(y.astype(BF16), wout_ref[...])
    pe = _dot(p.astype(BF16), wple_ref[...])
    h2 = h1 + _sigmoid(_dot(h1.astype(BF16), wpg_ref[...])) * pe
    return h2 * lax.rsqrt(jnp.mean(h2 * h2, axis=-1, keepdims=True) + EPS) * fg_ref[...]


def _hgrn_state_kernel(qf_ref, ig_ref, lb_ref, hng_ref, S0_ref, oh_ref, S_ref, st_sc, *, L, G):
    W = H_WIDTH
    for g in range(G):
        rows = pl.ds(g * L, L)
        st = st_sc.at[g]
        for h in range(H_HEADS):
            st[h] = S0_ref[g, h].T
        _hgrn_chunk(qf_ref[rows, 0:W].astype(F32), qf_ref[rows, W:2 * W].astype(F32),
                    ig_ref[rows, 0:W].astype(F32), ig_ref[rows, W:2 * W].astype(F32),
                    lb_ref, hng_ref, st, oh_ref.at[rows], L)
        for h in range(H_HEADS):
            S_ref[g, h] = st[h].T


def _hgrn_state(proj, w, S0, B, L, G):
    off = N_A // (2 * H_WIDTH)
    s_spec = pl.BlockSpec((G, H_HEADS, H_EXPAND, H_EXPAND), lambda b: (b, 0, 0, 0))
    return pl.pallas_call(
        functools.partial(_hgrn_state_kernel, L=L, G=G),
        grid=(B // G,),
        in_specs=[
            pl.BlockSpec((G * L, 2 * H_WIDTH), lambda b: (b, off)),
            pl.BlockSpec((G * L, 2 * H_WIDTH), lambda b: (b, off + 1)),
            _resident(w["hgrn_lb"].shape), _resident(w["hnorm_g"].shape), s_spec,
        ],
        out_specs=[pl.BlockSpec((G * L, H_WIDTH), lambda b: (b, 0)), s_spec],
        out_shape=[jax.ShapeDtypeStruct((B * L, H_WIDTH), BF16),
                   jax.ShapeDtypeStruct((B, H_HEADS, H_EXPAND, H_EXPAND), F32)],
        scratch_shapes=[pltpu.VMEM((G, H_HEADS, H_EXPAND, H_EXPAND), F32)],
        compiler_params=pltpu.CompilerParams(dimension_semantics=("arbitrary",), vmem_limit_bytes=VMEM_LIMIT),
        name="hgrn_state",
    )(proj, proj, w["hgrn_lb"], w["hnorm_g"], S0)


def _hgrn_kernel(x_ref, ng_ref, wb_ref, lb_ref, hng_ref, oh_ref, gab_ref, S_ref, st_sc, *, L):
    W = H_WIDTH

    @pl.when(pl.program_id(1) == 0)
    def _():
        st_sc[...] = jnp.zeros(st_sc.shape, F32)

    xh, _ = _rmsnorm_split(x_ref, ng_ref)
    q_pre = _dot_nt(xh, wb_ref[0:W, :])
    f = _dot_nt(xh, wb_ref[W:2 * W, :])
    v = _dot_nt(xh, wb_ref[2 * W:3 * W, :])
    gz = _dot_nt(xh, wb_ref[3 * W:4 * W, :])
    gab_ref[...] = _dot_nt(xh, wb_ref[4 * W:N_B, :]).astype(BF16)
    _hgrn_chunk(q_pre, f, v, gz, lb_ref, hng_ref, st_sc, oh_ref, L)

    @pl.when(pl.program_id(1) == pl.num_programs(1) - 1)
    def _():
        for h in range(H_HEADS):
            S_ref[h] = st_sc[h].T


def _hgrn(x2, w, B, T, L):
    NC = T // L
    tok = lambda b, c: (b * NC + c, 0)
    consts = [w["norm_g"], w["wt_b"], w["hgrn_lb"], w["hnorm_g"]]
    return pl.pallas_call(
        functools.partial(_hgrn_kernel, L=L),
        grid=(B, NC),
        in_specs=[pl.BlockSpec((L, D_MODEL), tok)] + [_resident(a.shape) for a in consts],
        out_specs=[pl.BlockSpec((L, H_WIDTH), tok), pl.BlockSpec((L, 2 * D_MODEL), tok),
                   pl.BlockSpec((None, H_HEADS, H_EXPAND, H_EXPAND), lambda b, c: (b, 0, 0, 0))],
        out_shape=[jax.ShapeDtypeStruct((B * T, H_WIDTH), BF16),
                   jax.ShapeDtypeStruct((B * T, 2 * D_MODEL), BF16),
                   jax.ShapeDtypeStruct((B, H_HEADS, H_EXPAND, H_EXPAND), F32)],
        scratch_shapes=[pltpu.VMEM((H_HEADS, H_EXPAND, H_EXPAND), F32)],
        compiler_params=pltpu.CompilerParams(
            dimension_semantics=("arbitrary", "arbitrary"), vmem_limit_bytes=VMEM_LIMIT),
        name="hgrn",
    )(x2, *consts)


def _merge_kernel(x_ref, p_ref, hm_ref, oh_ref, gab_ref, wbrm_ref, wbrh_ref, wout_ref, wpg_ref, wple_ref,
                  fg_ref, o_ref):
    o_ref[...] = _merge_math(x_ref[...], p_ref[...], hm_ref[...], oh_ref[...], gab_ref[...],
                             wbrm_ref, wbrh_ref, wout_ref, wpg_ref, wple_ref, fg_ref)


def _merge(x2, p2, hm, oh, gab, gab_blk, w, tm):
    n_tok = x2.shape[0]
    tok = lambda i: (i, 0)
    consts = [w["w_brm"], w["w_sq"], w["w_sq"], w["w_sq"], w["w_ple"], w["final_g"]]
    square = lambda k: pl.BlockSpec((None, D_MODEL, D_MODEL), lambda i: (k, 0, 0), pipeline_mode=pl.Buffered(1))
    return pl.pallas_call(
        _merge_kernel,
        grid=(n_tok // tm,),
        in_specs=[
            pl.BlockSpec((tm, D_MODEL), tok),
            pl.BlockSpec((tm, PLE_DIM), tok),
            pl.BlockSpec((tm, M_INNER), tok),
            pl.BlockSpec((tm, H_WIDTH), tok),
            pl.BlockSpec((tm, 2 * D_MODEL), lambda i: (i, gab_blk)),
            _resident(w["w_brm"].shape), square(0), square(1), square(2),
            _resident(w["w_ple"].shape), _resident(w["final_g"].shape),
        ],
        out_specs=pl.BlockSpec((tm, D_MODEL), tok),
        out_shape=jax.ShapeDtypeStruct((n_tok, D_MODEL), F32),
        compiler_params=pltpu.CompilerParams(
            dimension_semantics=("arbitrary",), vmem_limit_bytes=VMEM_LIMIT),
        name="merge",
    )(x2, p2, hm, oh, gab, *consts)


def _trunk(x, p, state, w):
    B, T, _ = x.shape
    x2 = x.reshape(B * T, D_MODEL)
    p2 = p.reshape(B * T, PLE_DIM)
    tm = min(MERGE_TILE, B * T)
    if state is None:
        L = min(PROMPT_CHUNK, T)
        hm, conv_n, C_n, n_n, m_n = _mlstm(x2, w, B, T, L)
        oh, gab, S_n = _hgrn(x2, w, B, T, L)
        y = _merge(x2, p2, hm, oh, gab, 0, w, tm)
    else:
        conv0, C0, n0, m0, S0 = state
        proj, gates = _proj(x2, w, min(PROJ_TILE, B * T), 2 * D_MODEL)
        m0p = jnp.zeros((B, SUBLANES, GATE_LANES), F32).at[:, :M_HEADS, :].set(m0[:, :, None])
        m_state = (conv0, C0, n0.reshape(B, M_HEADS, 1, M_HD), m0p)
        hm, conv_n, C_n, n_n, m_n = _mlstm_state(proj, gates, w, m_state, B, T, math.gcd(B, MLSTM_STREAMS))
        oh, S_n = _hgrn_state(proj, w, S0, B, T, math.gcd(B, HGRN_STREAMS))
        y = _merge(x2, p2, hm, oh, proj, (N_A + 4 * H_WIDTH) // (2 * D_MODEL), w, tm)
    return (y.reshape(B, T, D_MODEL), conv_n[None], C_n[None], n_n.reshape(B, M_HEADS, M_HD)[None],
            m_n[:, :M_HEADS, 0][None], S_n[None])


def _prep_weights(norm_g, w_in, b_ig, b_fg, conv_w, conv_b, w_qm, w_km, mnorm_g, m_skip, w_brm, hgrn_lb, hnorm_g,
                  w_brh, w_out, w_ple, w_pg, final_g):
    assert norm_g.shape[0] == 1, "single-layer trunk"
    g1 = N_A + 2 * M_HEADS
    wt = jnp.swapaxes(w_in[0], 0, 1)
    wt_gate = wt[N_A:N_A + GATE_LANES]
    wt_all = wt.astype(BF16)
    return {
        "norm_g": norm_g[0][None, :],
        "wt_all": wt_all,
        "wt_b": wt_all[g1:],
        "wgt_lo": (wt_gate - wt_gate.astype(BF16).astype(F32)).astype(BF16),
        "gbias": jnp.pad(jnp.concatenate([b_ig[0], b_fg[0]]), (0, GATE_LANES - 2 * M_HEADS))[None, :],
        "conv_w": conv_w[0],
        "conv_b": conv_b[0][None, :],
        "w_qm": w_qm[0].astype(BF16),
        "w_km": w_km[0].astype(BF16),
        "mnorm_g": mnorm_g[0][None, :],
        "m_skip": m_skip[0][None, :],
        "w_brm": w_brm[0].astype(BF16),
        "hgrn_lb": hgrn_lb,
        "hnorm_g": hnorm_g[0][None, :],
        "w_sq": jnp.stack([w_brh[0], w_out[0], w_pg[0]]).astype(BF16),
        "w_ple": w_ple[0].astype(BF16),
        "final_g": final_g[None, :],
    }


def kernel(x_prompt, x_sample, state_conv, state_mlstm_C, state_mlstm_n, state_mlstm_m, state_hgrn, p_prompt, p_sample, norm_g, w_in, b_ig, b_fg, conv_w, conv_b, w_qm, w_km, mnorm_g, m_skip, w_brm, hgrn_lb, hnorm_g, w_brh, w_out, w_ple, w_pg, final_g):
    w = _prep_weights(norm_g, w_in, b_ig, b_fg, conv_w, conv_b, w_qm, w_km, mnorm_g, m_skip, w_brm, hgrn_lb,
                      hnorm_g, w_brh, w_out, w_ple, w_pg, final_g)
    out_p = _trunk(x_prompt, p_prompt[0], None, w)
    state = (state_conv[0], state_mlstm_C[0], state_mlstm_n[0], state_mlstm_m[0], state_hgrn[0])
    out_s = _trunk(x_sample, p_sample[0], state, w)
    return (out_p[0], out_s[0]) + tuple(out_p[1:]) + tuple(out_s[1:])
```
